```python
import math
import jax, jax.numpy as jnp
from jax import lax
import numpy as np

D_MODEL = 2048
BATCH = 4
SEQ = 2048
DEPTH = 1

CHUNK = 64
D_MIX = 2048
ATT_HEADS = 8
ATT_HEAD_DIM = 128
ATT_WIDTH = ATT_HEADS * ATT_HEAD_DIM
Q_BLOCK = 128
SSD_WIDTH = D_MIX - ATT_WIDTH
SSD_HEAD_DIM = 64
SSD_HEADS = SSD_WIDTH // SSD_HEAD_DIM
SSD_GROUPS = 2
SSD_STATE = 128
CONV_WIDTH = 4
CONV_CH = SSD_WIDTH + 2 * SSD_GROUPS * SSD_STATE
N_EXPERTS = 32
TOP_K = 4
D_FF = D_MODEL
SWIGLU_ALPHA = 1.702
SWIGLU_LIMIT = 7.0
MOE_BLOCK = 256
EPS = 1e-6
IN_COLS = 3 * ATT_WIDTH + ATT_HEADS + SSD_WIDTH + CONV_CH + SSD_HEADS

kernel_name = "hybrid_fox_ssd_moe_block"


def _rms_norm(x, g):
    x32 = x.astype(jnp.float32)
    y = x32 * lax.rsqrt(jnp.mean(x32 * x32, axis=-1, keepdims=True) + EPS)
    return (y * g.astype(jnp.float32)).astype(x.dtype)


def _forgetting_attention(q, k, v, log_f):
    b, s, h, d = q.shape
    nb = s // Q_BLOCK
    scale = 1.0 / math.sqrt(d)
    cum = jnp.cumsum(log_f.astype(jnp.float32), axis=1).transpose(0, 2, 1)
    qh = (q * scale).transpose(0, 2, 1, 3)
    kh = k.transpose(0, 2, 1, 3)
    vh = v.transpose(0, 2, 1, 3)
    q_blocks = qh.reshape(b, h, nb, Q_BLOCK, d).transpose(2, 0, 1, 3, 4)
    c_blocks = cum.reshape(b, h, nb, Q_BLOCK).transpose(2, 0, 1, 3)
    key_pos = jnp.arange(s)

    def one_block(args):
        qb, cb, i = args
        logits = jnp.einsum("bhqd,bhkd->bhqk", qb, kh).astype(jnp.float32)
        logits = logits + cb[..., None] - cum[:, :, None, :]
        q_pos = i * Q_BLOCK + jnp.arange(Q_BLOCK)
        mask = key_pos[None, :] <= q_pos[:, None]
        logits = jnp.where(mask, logits, -jnp.inf)
        p = jax.nn.softmax(logits, axis=-1)
        return jnp.einsum("bhqk,bhkd->bhqd", p.astype(vh.dtype), vh)

    out = lax.map(one_block, (q_blocks, c_blocks, jnp.arange(nb)))
    return out.transpose(1, 0, 3, 2, 4).reshape(b, s, h * d)


def _ssd_mixer(xbc, z, dt_raw, conv_w, conv_b, dt_bias, a_log, d_skip, norm_g):
    b, s, _ = xbc.shape
    H, P, G, N, L = SSD_HEADS, SSD_HEAD_DIM, SSD_GROUPS, SSD_STATE, CHUNK
    R = H // G
    nc = s // L
    xp = jnp.pad(xbc, ((0, 0), (CONV_WIDTH - 1, 0), (0, 0)))
    conv = conv_b
    for tap in range(CONV_WIDTH):
        conv = conv + xp[:, tap:tap + s, :] * conv_w[tap]
    conv = jax.nn.silu(conv)
    xs = conv[..., :SSD_WIDTH].reshape(b, s, H, P)
    Bm = conv[..., SSD_WIDTH:SSD_WIDTH + G * N].reshape(b, nc, L, G, N)
    Cm = conv[..., SSD_WIDTH + G * N:].reshape(b, nc, L, G, N)
    dt = jax.nn.softplus(dt_raw.astype(jnp.float32) + dt_bias.astype(jnp.float32))
    A = -jnp.exp(a_log.astype(jnp.float32))
    dA = (dt * A).reshape(b, nc, L, G, R)
    X = (xs * dt[..., None]).reshape(b, nc, L, G, R, P)
    a_cs = jnp.moveaxis(jnp.cumsum(dA, axis=2), 2, -1)
    diff = a_cs[..., :, None] - a_cs[..., None, :]
    tril = jnp.tril(jnp.ones((L, L), dtype=bool))
    Lmat = jnp.exp(jnp.where(tril, diff, -jnp.inf))
    CB = jnp.einsum("bclgn,bcsgn->bcgls", Cm, Bm)
    W = CB[:, :, :, None] * Lmat
    y_diag = jnp.einsum("bcgrls,bcsgrp->bclgrp", W, X)
    decay_states = jnp.exp(a_cs[..., -1:] - a_cs)
    states = jnp.einsum("bclgn,bcgrl,bclgrp->bcgrpn", Bm, decay_states, X)
    chunk_decay = jnp.exp(a_cs[..., -1])

    def step(hstate, inp):
        st, dec = inp
        return hstate * dec[..., None, None] + st, hstate

    _, h_prev = lax.scan(step, jnp.zeros_like(states[:, 0]),
                         (jnp.moveaxis(states, 1, 0), jnp.moveaxis(chunk_decay, 1, 0)))
    h_prev = jnp.moveaxis(h_prev, 0, 1)
    y_off = jnp.einsum("bclgn,bcgrpn,bcgrl->bclgrp", Cm, h_prev, jnp.exp(a_cs))
    y = (y_diag + y_off).reshape(b, s, H, P) + xs * d_skip[:, None]
    y = y.reshape(b, s, SSD_WIDTH)
    return _rms_norm(y * jax.nn.silu(z), norm_g)


def _moe(h, layer, w_router, b_router, w_gate, b_gate, w_up, b_up, w_down, b_down):
    b, s, d = h.shape
    n_tok = b * s
    nk = n_tok * TOP_K
    hf = h.reshape(n_tok, d)
    logits = (hf @ w_router[layer] + b_router[layer]).astype(jnp.float32)
    top_vals, top_idx = lax.top_k(logits, TOP_K)
    gates = jax.nn.softmax(top_vals, axis=-1)
    flat_e = top_idx.reshape(-1)
    flat_tok = jnp.repeat(jnp.arange(n_tok, dtype=jnp.int32), TOP_K)
    flat_w = gates.reshape(-1)
    order = jnp.argsort(flat_e)
    sorted_e = flat_e[order]
    sorted_tok = flat_tok[order]
    sorted_w = flat_w[order]
    counts = jnp.bincount(flat_e, length=N_EXPERTS)
    starts = jnp.cumsum(counts) - counts
    padded = ((counts + MOE_BLOCK - 1) // MOE_BLOCK) * MOE_BLOCK
    pend = jnp.cumsum(padded)
    pstart = pend - padded
    rank = jnp.arange(nk) - starts[sorted_e]
    dest = pstart[sorted_e] + rank
    n_blocks = -(-nk // MOE_BLOCK) + N_EXPERTS
    n_slots = n_blocks * MOE_BLOCK
    slot_tok = jnp.full((n_slots,), n_tok, dtype=jnp.int32).at[dest].set(sorted_tok)
    slot_w = jnp.zeros((n_slots,), dtype=jnp.float32).at[dest].set(sorted_w)
    block_start = jnp.arange(n_blocks) * MOE_BLOCK
    block_e = jnp.minimum(jnp.searchsorted(pend, block_start, side="right"), N_EXPERTS - 1)
    h_pad = jnp.concatenate([hf, jnp.zeros((1, d), hf.dtype)], axis=0)
    xb = h_pad[slot_tok].reshape(n_blocks, MOE_BLOCK, d)

    def expert_block(args):
        xblk, e = args
        g = xblk @ w_gate[layer, e] + b_gate[layer, e]
        u = xblk @ w_up[layer, e] + b_up[layer, e]
        g = jnp.minimum(g, SWIGLU_LIMIT)
        u = jnp.clip(u, -SWIGLU_LIMIT, SWIGLU_LIMIT)
        glu = g * jax.nn.sigmoid(SWIGLU_ALPHA * g)
        return ((u + 1.0) * glu) @ w_down[layer, e] + b_down[layer, e]

    yb = lax.map(expert_block, (xb, block_e)).reshape(n_slots, d) * slot_w[:, None]
    y = jax.ops.segment_sum(yb, slot_tok, num_segments=n_tok + 1)[:n_tok]
    return y.reshape(b, s, d).astype(h.dtype)


def setup_inputs(seed: int = 0) -> dict:
    key = jax.random.key(seed)
    ks = jax.random.split(key, 24)
    f32 = jnp.float32
    Ld = DEPTH

    def nrm(k, shape, scale):
        return jax.random.normal(k, shape, f32) * scale

    x = nrm(ks[0], (BATCH, SEQ, D_MODEL), 1.0)
    attn_norm_g = 1.0 + nrm(ks[1], (Ld, D_MODEL), 0.02)
    w_in = nrm(ks[2], (Ld, D_MODEL, IN_COLS), D_MODEL ** -0.5)
    forget_bias = jax.random.uniform(ks[3], (Ld, ATT_HEADS), f32, 2.5, 4.5)
    q_norm_g = 1.0 + nrm(ks[4], (Ld, ATT_HEAD_DIM), 0.02)
    k_norm_g = 1.0 + nrm(ks[5], (Ld, ATT_HEAD_DIM), 0.02)
    conv_w = nrm(ks[6], (Ld, CONV_WIDTH, CONV_CH), CONV_WIDTH ** -0.5)
    conv_b = nrm(ks[7], (Ld, CONV_CH), 0.01)
    dt0 = jnp.exp(jax.random.uniform(ks[8], (Ld, SSD_HEADS), f32, math.log(1e-3), math.log(1e-1)))
    dt_bias = dt0 + jnp.log(-jnp.expm1(-dt0))
    a_log = jnp.log(jax.random.uniform(ks[9], (Ld, SSD_HEADS), f32, 1.0, 16.0))
    d_skip = 1.0 + nrm(ks[10], (Ld, SSD_HEADS), 0.02)
    ssd_norm_g = 1.0 + nrm(ks[11], (Ld, SSD_WIDTH), 0.02)
    w_out = nrm(ks[12], (Ld, D_MIX, D_MODEL), D_MIX ** -0.5)
    ffn_norm_g = 1.0 + nrm(ks[13], (Ld, D_MODEL), 0.02)
    w_router = nrm(ks[14], (Ld, D_MODEL, N_EXPERTS), D_MODEL ** -0.5)
    b_router = nrm(ks[15], (Ld, N_EXPERTS), 0.01)
    w_gate = nrm(ks[16], (Ld, N_EXPERTS, D_MODEL, D_FF), D_MODEL ** -0.5)
    b_gate = nrm(ks[17], (Ld, N_EXPERTS, D_FF), 0.01)
    w_up = nrm(ks[18], (Ld, N_EXPERTS, D_MODEL, D_FF), D_MODEL ** -0.5)
    b_up = nrm(ks[19], (Ld, N_EXPERTS, D_FF), 0.01)
    w_down = nrm(ks[20], (Ld, N_EXPERTS, D_FF, D_MODEL), D_FF ** -0.5)
    b_down = nrm(ks[21], (Ld, N_EXPERTS, D_MODEL), 0.01)
    return {"x": x, "attn_norm_g": attn_norm_g, "w_in": w_in, "forget_bias": forget_bias,
            "q_norm_g": q_norm_g, "k_norm_g": k_norm_g, "conv_w": conv_w, "conv_b": conv_b,
            "dt_bias": dt_bias, "a_log": a_log, "d_skip": d_skip, "ssd_norm_g": ssd_norm_g,
            "w_out": w_out, "ffn_norm_g": ffn_norm_g, "w_router": w_router, "b_router": b_router,
            "w_gate": w_gate, "b_gate": b_gate, "w_up": w_up, "b_up": b_up,
            "w_down": w_down, "b_down": b_down}


def reference(x, attn_norm_g, w_in, forget_bias, q_norm_g, k_norm_g, conv_w, conv_b,
              dt_bias, a_log, d_skip, ssd_norm_g, w_out, ffn_norm_g, w_router, b_router,
              w_gate, b_gate, w_up, b_up, w_down, b_down):
    b, s, _ = x.shape
    o1 = ATT_WIDTH
    o2 = o1 + ATT_WIDTH
    o3 = o2 + ATT_WIDTH
    o4 = o3 + ATT_HEADS
    o5 = o4 + SSD_WIDTH
    o6 = o5 + CONV_CH
    for layer in range(DEPTH):
        h = _rms_norm(x, attn_norm_g[layer])
        proj = h @ w_in[layer]
        q, k, v, f_raw, z, xbc, dt_raw = jnp.split(proj, [o1, o2, o3, o4, o5, o6], axis=-1)
        q = _rms_norm(q.reshape(b, s, ATT_HEADS, ATT_HEAD_DIM), q_norm_g[layer])
        k = _rms_norm(k.reshape(b, s, ATT_HEADS, ATT_HEAD_DIM), k_norm_g[layer])
        v = v.reshape(b, s, ATT_HEADS, ATT_HEAD_DIM)
        log_f = jax.nn.log_sigmoid(f_raw.astype(jnp.float32) + forget_bias[layer].astype(jnp.float32))
        att = _forgetting_attention(q, k, v, log_f)
        ssd = _ssd_mixer(xbc, z, dt_raw, conv_w[layer], conv_b[layer], dt_bias[layer],
                         a_log[layer], d_skip[layer], ssd_norm_g[layer])
        mix = jnp.concatenate([att.astype(x.dtype), ssd.astype(x.dtype)], axis=-1)
        x = x + (mix @ w_out[layer]).astype(x.dtype)
        h = _rms_norm(x, ffn_norm_g[layer])
        x = x + _moe(h, layer, w_router, b_router, w_gate, b_gate, w_up, b_up, w_down, b_down)
    return x
```

```python
import functools
import math

import jax
import jax.numpy as jnp
from jax import lax
from jax.experimental import pallas as pl
from jax.experimental.pallas import tpu as pltpu

F32 = jnp.float32
BF16 = jnp.bfloat16

ATT_HEADS = 8
ATT_HEAD_DIM = 128
SSD_HEAD_DIM = 64
SSD_GROUPS = 2
SSD_STATE = 128
CONV_WIDTH = 4
N_EXPERTS = 32
TOP_K = 4
SWIGLU_ALPHA = 1.702
SWIGLU_LIMIT = 7.0
EPS = 1e-6

V7X_LANES = 128
V7X_SUBLANES = 8
V7X_VMEM_BYTES = 64 * 1024 * 1024
VMEM_LIMIT_BYTES = 56 * 1024 * 1024

PROJ_ROWS = 256
PROJ_COLS = 512
CUM_ROWS = 256
ATT_BLOCK = 512
SSD_ROWS = 256
MOE_BLOCK = 256
FF_CHUNK = 512
COMB_ROWS = 128
SMALL_COLS = V7X_LANES


def _params(dims):
    return pltpu.CompilerParams(dimension_semantics=dims, vmem_limit_bytes=VMEM_LIMIT_BYTES)


def _rms(x, g, eps):
    return x * lax.rsqrt(jnp.mean(x * x, axis=-1, keepdims=True) + eps) * g


def _split3(x):
    x1 = x.astype(BF16)
    r1 = x - x1.astype(F32)
    x2 = r1.astype(BF16)
    r2 = r1 - x2.astype(F32)
    return x1, x2, r2.astype(BF16)


def _dot(a, b):
    return jnp.dot(a, b, preferred_element_type=F32)


def _tri_cumsum(x, n):
    row = lax.broadcasted_iota(jnp.int32, (n, n), 0)
    col = lax.broadcasted_iota(jnp.int32, (n, n), 1)
    tri = (row >= col).astype(BF16)
    x1, x2, x3 = _split3(x)
    return _dot(tri, x1) + _dot(tri, x2) + _dot(tri, x3)


def _softplus(x):
    return jnp.maximum(x, 0.0) + jnp.log1p(jnp.exp(-jnp.abs(x)))


def _silu(x):
    return x * (1.0 / (1.0 + jnp.exp(-x)))


def _in_proj_body(x_ref, g_ref, w_ref, qg_ref, kg_ref,
                  q_ref, k_ref, v_ref, z_ref, xbc_ref, sm_ref, *, head_dim, q_scale):
    att_w = q_ref.shape[-1]
    ssd_w = z_ref.shape[-1]
    conv_ch = xbc_ref.shape[-1]
    h = _rms(x_ref[...], g_ref[...], EPS).astype(BF16)

    def proj(c0, width):
        return _dot(h, w_ref[:, c0:c0 + width])

    def chunks(width):
        return [(c, min(PROJ_COLS, width - c)) for c in range(0, width, PROJ_COLS)]

    col = 0
    for out_ref, gain_ref, scale in ((q_ref, qg_ref, q_scale), (k_ref, kg_ref, None)):
        for c0, cw in chunks(att_w):
            y = proj(col + c0, cw)
            for j in range(cw // head_dim):
                yn = _rms(y[:, j * head_dim:(j + 1) * head_dim], gain_ref[...], EPS)
                if scale is not None:
                    yn = yn * scale
                out_ref[:, c0 + j * head_dim:c0 + (j + 1) * head_dim] = yn.astype(out_ref.dtype)
        col += att_w
    for out_ref, width in ((v_ref, att_w), (z_ref, ssd_w), (xbc_ref, conv_ch), (sm_ref, SMALL_COLS)):
        for c0, cw in chunks(width):
            out_ref[:, c0:c0 + cw] = proj(col + c0, cw).astype(out_ref.dtype)
        col += width


def _in_proj(x2d, g, w_all, qg, kg, *, att_w, ssd_w, conv_ch):
    n, d = x2d.shape
    tm = PROJ_ROWS
    row = lambda i: (i, 0)
    fixed = lambda i: (0, 0)
    outs = [(att_w, BF16), (att_w, BF16), (att_w, BF16), (ssd_w, F32), (conv_ch, F32), (SMALL_COLS, F32)]
    return pl.pallas_call(
        functools.partial(_in_proj_body, head_dim=ATT_HEAD_DIM, q_scale=1.0 / math.sqrt(ATT_HEAD_DIM)),
        grid=(n // tm,),
        in_specs=[
            pl.BlockSpec((tm, d), row),
            pl.BlockSpec((1, d), fixed),
            pl.BlockSpec(w_all.shape, fixed, pipeline_mode=pl.Buffered(1)),
            pl.BlockSpec((1, ATT_HEAD_DIM), fixed),
            pl.BlockSpec((1, ATT_HEAD_DIM), fixed),
        ],
        out_specs=[pl.BlockSpec((tm, w), row) for w, _ in outs],
        out_shape=[jax.ShapeDtypeStruct((n, w), dt) for w, dt in outs],
        compiler_params=_params(("arbitrary",)),
        name="in_proj",
    )(x2d, g, w_all, qg, kg)


def _forget_cum_body(sm_ref, fb_ref, cs_ref, ct_ref, carry_ref):
    @pl.when(pl.program_id(1) == 0)
    def _():
        carry_ref[...] = jnp.zeros_like(carry_ref)

    n = sm_ref.shape[1]
    log_f = -_softplus(-(sm_ref[0] + fb_ref[...]))
    c = _tri_cumsum(log_f, n) + carry_ref[...]
    cs_ref[0] = c
    ct_ref[0] = c.T[0:ct_ref.shape[1], :]
    carry_ref[...] = c[n - 1:n, :]


def _forget_cum(small, fb):
    b, s, w = small.shape
    tc = CUM_ROWS
    return pl.pallas_call(
        _forget_cum_body,
        grid=(b, s // tc),
        in_specs=[pl.BlockSpec((1, tc, w), lambda i, j: (i, j, 0)),
                  pl.BlockSpec((1, w), lambda i, j: (0, 0))],
        out_specs=[pl.BlockSpec((1, tc, w), lambda i, j: (i, j, 0)),
                   pl.BlockSpec((1, V7X_SUBLANES, tc), lambda i, j: (i, 0, j))],
        out_shape=[jax.ShapeDtypeStruct((b, s, w), F32),
                   jax.ShapeDtypeStruct((b, V7X_SUBLANES, s), F32)],
        scratch_shapes=[pltpu.VMEM((1, w), F32)],
        compiler_params=_params(("arbitrary", "arbitrary")),
        name="forget_cum",
    )(small, fb)


def _attn_body(qi_ref, ki_ref, q_ref, k_ref, v_ref, cs_ref, ct_ref, o_ref,
               m_ref, l_ref, acc_ref, cq_ref):
    head = pl.program_id(1)
    p = pl.program_id(2)
    qi = qi_ref[p]
    ki = ki_ref[p]
    tq = q_ref.shape[1]
    tk = k_ref.shape[1]

    @pl.when(ki == 0)
    def _():
        m_ref[...] = jnp.full_like(m_ref, -jnp.inf)
        l_ref[...] = jnp.zeros_like(l_ref)
        acc_ref[...] = jnp.zeros_like(acc_ref)
        lane = lax.broadcasted_iota(jnp.int32, cs_ref.shape[1:], 1)
        cq_ref[...] = jnp.sum(jnp.where(lane == head, cs_ref[0], 0.0), axis=-1, keepdims=True)

    s = lax.dot_general(q_ref[0], k_ref[0], (((1,), (1,)), ((), ())), preferred_element_type=F32)
    ck = ct_ref[0, pl.ds(head, 1), :]
    s = (s + cq_ref[...]) - ck
    q_pos = qi * tq + lax.broadcasted_iota(jnp.int32, (tq, tk), 0)
    k_pos = ki * tk + lax.broadcasted_iota(jnp.int32, (tq, tk), 1)
    s = jnp.where(k_pos <= q_pos, s, -jnp.inf)

    m_old = m_ref[...]
    m_new = jnp.maximum(m_old, jnp.max(s, axis=-1, keepdims=True))
    alpha = jnp.exp(m_old - m_new)
    pmat = jnp.exp(s - m_new)
    l_ref[...] = alpha * l_ref[...] + jnp.sum(pmat, axis=-1, keepdims=True)
    acc_ref[...] = alpha * acc_ref[...] + _dot(pmat.astype(BF16), v_ref[0])
    m_ref[...] = m_new

    @pl.when(ki == qi)
    def _():
        o_ref[0] = (acc_ref[...] / l_ref[...]).astype(o_ref.dtype)


def _attention(q, k, v, cum_s, cum_t):
    b, s, w = q.shape
    hd = ATT_HEAD_DIM
    heads = w // hd
    t = ATT_BLOCK
    nq = s // t
    pairs = [(i, j) for i in range(nq) for j in range(i + 1)]
    qi = jnp.asarray([p[0] for p in pairs], jnp.int32)
    ki = jnp.asarray([p[1] for p in pairs], jnp.int32)
    grid_spec = pltpu.PrefetchScalarGridSpec(
        num_scalar_prefetch=2,
        grid=(b, heads, len(pairs)),
        in_specs=[
            pl.BlockSpec((1, t, hd), lambda bi, h, p, qi, ki: (bi, qi[p], h)),
            pl.BlockSpec((1, t, hd), lambda bi, h, p, qi, ki: (bi, ki[p], h)),
            pl.BlockSpec((1, t, hd), lambda bi, h, p, qi, ki: (bi, ki[p], h)),
            pl.BlockSpec((1, t, cum_s.shape[-1]), lambda bi, h, p, qi, ki: (bi, qi[p], 0)),
            pl.BlockSpec((1, cum_t.shape[1], t), lambda bi, h, p, qi, ki: (bi, 0, ki[p])),
        ],
        out_specs=pl.BlockSpec((1, t, hd), lambda bi, h, p, qi, ki: (bi, qi[p], h)),
        scratch_shapes=[pltpu.VMEM((t, 1), F32), pltpu.VMEM((t, 1), F32),
                        pltpu.VMEM((t, hd), F32), pltpu.VMEM((t, 1), F32)],
    )
    return pl.pallas_call(
        _attn_body,
        grid_spec=grid_spec,
        out_shape=jax.ShapeDtypeStruct((b, s, w), BF16),
        compiler_params=_params(("arbitrary", "arbitrary", "arbitrary")),
        name="attention",
    )(qi, ki, q, k, v, cum_s, cum_t)


def _ssd_body(xbc_ref, z_ref, sm_ref, cw_ref, cb_ref, dtb_ref, alog_ref, dskip_ref, ng_ref,
              o_ref, ext_ref, state_ref, y_ref, *, dt_lane0):
    t = xbc_ref.shape[1]
    width = z_ref.shape[-1]
    lanes = V7X_LANES
    n_pairs = width // lanes
    pairs_per_group = n_pairs // SSD_GROUPS
    tail = V7X_SUBLANES

    @pl.when(pl.program_id(1) == 0)
    def _():
        ext_ref[0:tail, :] = jnp.zeros((tail, ext_ref.shape[1]), F32)
        state_ref[...] = jnp.zeros_like(state_ref)

    ext_ref[tail:tail + t, :] = xbc_ref[0]
    conv = cb_ref[...]
    for tap in range(CONV_WIDTH):
        off = tail - (CONV_WIDTH - 1) + tap
        conv = conv + ext_ref[off:off + t, :] * cw_ref[tap:tap + 1, :]
    ext_ref[0:tail, :] = ext_ref[t:t + tail, :]
    conv = _silu(conv)

    sm = sm_ref[0]
    dt = _softplus(sm + dtb_ref[...])
    d_a = dt * (-jnp.exp(alog_ref[...]))
    a_cs = _tri_cumsum(d_a, t)
    a_cs_t = a_cs.T
    a_last = a_cs[t - 1:t, :]

    row = lax.broadcasted_iota(jnp.int32, (t, t), 0)
    col = lax.broadcasted_iota(jnp.int32, (t, t), 1)
    causal = row >= col
    lane = lax.broadcasted_iota(jnp.int32, (t, lanes), 1)
    first_half = lane < SSD_HEAD_DIM
    lane1 = lax.broadcasted_iota(jnp.int32, (1, lanes), 1)

    b_off = width
    c_off = width + SSD_GROUPS * SSD_STATE
    sumsq = jnp.zeros((t, 1), F32)
    for g in range(SSD_GROUPS):
        b_g = conv[:, b_off + g * SSD_STATE:b_off + (g + 1) * SSD_STATE].astype(BF16)
        c_g = conv[:, c_off + g * SSD_STATE:c_off + (g + 1) * SSD_STATE].astype(BF16)
        cb = lax.dot_general(c_g, b_g, (((1,), (1,)), ((), ())), preferred_element_type=F32)
        for jp in range(pairs_per_group):
            pr = g * pairs_per_group + jp
            h0 = dt_lane0 + 2 * pr
            xs = conv[:, pr * lanes:(pr + 1) * lanes]
            a_pair = jnp.where(first_half, a_cs[:, h0:h0 + 1], a_cs[:, h0 + 1:h0 + 2])
            dt_pair = jnp.where(first_half, dt[:, h0:h0 + 1], dt[:, h0 + 1:h0 + 2])
            a_last_pair = jnp.where(lane1 < SSD_HEAD_DIM, a_last[:, h0:h0 + 1], a_last[:, h0 + 1:h0 + 2])
            x_dt = xs * dt_pair
            y = jnp.zeros((t, lanes), F32)
            for half in range(2):
                hh = h0 + half
                diff = a_cs[:, hh:hh + 1] - a_cs_t[hh:hh + 1, :]
                wmat = cb * jnp.exp(jnp.where(causal, diff, -jnp.inf))
                keep = first_half if half == 0 else jnp.logical_not(first_half)
                y = y + _dot(wmat.astype(BF16), jnp.where(keep, x_dt, 0.0).astype(BF16))
            st = state_ref[pr]
            y = y + _dot(c_g, st.astype(BF16)) * jnp.exp(a_pair)
            x_dec = (x_dt * jnp.exp(a_last_pair - a_pair)).astype(BF16)
            upd = lax.dot_general(b_g, x_dec, (((0,), (0,)), ((), ())), preferred_element_type=F32)
            state_ref[pr] = st * jnp.exp(a_last_pair) + upd
            y = y + xs * dskip_ref[:, pr * lanes:(pr + 1) * lanes]
            zz = z_ref[0, :, pr * lanes:(pr + 1) * lanes]
            y = y * _silu(zz)
            sumsq = sumsq + jnp.sum(y * y, axis=-1, keepdims=True)
            y_ref[:, pr * lanes:(pr + 1) * lanes] = y
    scale = lax.rsqrt(sumsq / width + EPS)
    o_ref[0] = (y_ref[...] * scale * ng_ref[...]).astype(o_ref.dtype)


def _ssd(xbc, z, small, conv_w, conv_b, dtb, alog, dskip, norm_g, *, dt_lane0):
    b, s, cc = xbc.shape
    width = z.shape[-1]
    t = SSD_ROWS
    n_pairs = width // V7X_LANES
    blk = lambda i, j: (i, j, 0)
    fixed = lambda i, j: (0, 0)
    return pl.pallas_call(
        functools.partial(_ssd_body, dt_lane0=dt_lane0),
        grid=(b, s // t),
        in_specs=[
            pl.BlockSpec((1, t, cc), blk),
            pl.BlockSpec((1, t, width), blk),
            pl.BlockSpec((1, t, small.shape[-1]), blk),
            pl.BlockSpec(conv_w.shape, fixed),
            pl.BlockSpec(conv_b.shape, fixed),
            pl.BlockSpec(dtb.shape, fixed),
            pl.BlockSpec(alog.shape, fixed),
            pl.BlockSpec(dskip.shape, fixed),
            pl.BlockSpec(norm_g.shape, fixed),
        ],
        out_specs=pl.BlockSpec((1, t, width), blk),
        out_shape=jax.ShapeDtypeStruct((b, s, width), BF16),
        scratch_shapes=[
            pltpu.VMEM((t + V7X_SUBLANES, cc), F32),
            pltpu.VMEM((n_pairs, SSD_STATE, V7X_LANES), F32),
            pltpu.VMEM((t, width), F32),
        ],
        compiler_params=_params(("arbitrary", "arbitrary")),
        name="ssd",
    )(xbc, z, small, conv_w, conv_b, dtb, alog, dskip, norm_g)


def _out_proj_body(att_ref, ssd_ref, x_ref, wo_ref, g_ref, wrh_ref, wrl_ref, br_ref,
                   x1_ref, idx_ref, gate_ref):
    aw = att_ref.shape[-1]
    d = x_ref.shape[-1]
    tm = x_ref.shape[0]
    for c0 in range(0, d, PROJ_COLS):
        y = _dot(att_ref[...], wo_ref[0:aw, c0:c0 + PROJ_COLS])
        y = y + _dot(ssd_ref[...], wo_ref[aw:, c0:c0 + PROJ_COLS])
        x1_ref[:, c0:c0 + PROJ_COLS] = x_ref[:, c0:c0 + PROJ_COLS] + y
    h2 = _rms(x1_ref[...], g_ref[...], EPS)
    h_hi = h2.astype(BF16)
    h_lo = (h2 - h_hi.astype(F32)).astype(BF16)
    logits = _dot(h_hi, wrh_ref[...]) + _dot(h_lo, wrh_ref[...]) + _dot(h_hi, wrl_ref[...]) + br_ref[...]
    lane = lax.broadcasted_iota(jnp.int32, (tm, V7X_LANES), 1)
    logits = jnp.where(lane < N_EXPERTS, logits, -jnp.inf)
    lane_f = lane.astype(F32)
    idx_out = jnp.zeros((tm, V7X_LANES), F32)
    vals = []
    for kk in range(TOP_K):
        m = jnp.max(logits, axis=-1, keepdims=True)
        sel = jnp.min(jnp.where(logits == m, lane_f, float(V7X_LANES)), axis=-1, keepdims=True)
        idx_out = jnp.where(lane == kk, sel, idx_out)
        logits = jnp.where(lane_f == sel, -jnp.inf, logits)
        vals.append(m)
    exps = [jnp.exp(v - vals[0]) for v in vals]
    denom = exps[0]
    for e in exps[1:]:
        denom = denom + e
    gate_out = jnp.zeros((tm, V7X_LANES), F32)
    for kk in range(TOP_K):
        gate_out = jnp.where(lane == kk, exps[kk] / denom, gate_out)
    idx_ref[...] = idx_out.astype(jnp.int32)
    gate_ref[...] = gate_out


def _out_proj(att, ssd, x2d, wo, g, wr_hi, wr_lo, br):
    n, d = x2d.shape
    tm = PROJ_ROWS
    row = lambda i: (i, 0)
    fixed = lambda i: (0, 0)
    return pl.pallas_call(
        _out_proj_body,
        grid=(n // tm,),
        in_specs=[
            pl.BlockSpec((tm, att.shape[-1]), row),
            pl.BlockSpec((tm, ssd.shape[-1]), row),
            pl.BlockSpec((tm, d), row),
            pl.BlockSpec(wo.shape, fixed, pipeline_mode=pl.Buffered(1)),
            pl.BlockSpec((1, d), fixed),
            pl.BlockSpec(wr_hi.shape, fixed),
            pl.BlockSpec(wr_lo.shape, fixed),
            pl.BlockSpec((1, V7X_LANES), fixed),
        ],
        out_specs=[pl.BlockSpec((tm, d), row),
                   pl.BlockSpec((tm, V7X_LANES), row),
                   pl.BlockSpec((tm, V7X_LANES), row)],
        out_shape=[jax.ShapeDtypeStruct((n, d), F32),
                   jax.ShapeDtypeStruct((n, V7X_LANES), jnp.int32),
                   jax.ShapeDtypeStruct((n, V7X_LANES), F32)],
        compiler_params=_params(("arbitrary",)),
        name="out_proj",
    )(att, ssd, x2d, wo, g, wr_hi, wr_lo, br)


def _row_gather(tok_ref, src_hbm, dst_ref, sem, n_rows):
    def body(r, carry):
        pltpu.make_async_copy(src_hbm.at[pl.ds(tok_ref[0, 0, r], 1), :],
                              dst_ref.at[pl.ds(r, 1), :], sem).start()
        return carry
    lax.fori_loop(0, n_rows, body, 0)


def _experts_body(be_ref, nb_ref, tok_ref, x1_hbm, g_ref, wg_ref, bg_ref, wu_ref, bu_ref, wd_ref, bd_ref,
                  o_ref, xraw_ref, xs_ref, sem):
    i = pl.program_id(0)
    c = pl.program_id(1)
    used = i < nb_ref[0]
    rows = xraw_ref.shape[0]

    @pl.when(jnp.logical_and(used, c == 0))
    def _():
        _row_gather(tok_ref, x1_hbm, xraw_ref, sem, rows)
        pltpu.make_async_copy(xraw_ref, xraw_ref, sem).wait()
        xs_ref[...] = _rms(xraw_ref[...], g_ref[...], EPS).astype(BF16)
        o_ref[...] = jnp.broadcast_to(bd_ref[0], o_ref.shape)

    @pl.when(used)
    def _():
        xs = xs_ref[...]
        gg = _dot(xs, wg_ref[0].astype(BF16)) + bg_ref[0]
        uu = _dot(xs, wu_ref[0].astype(BF16)) + bu_ref[0]
        gg = jnp.minimum(gg, SWIGLU_LIMIT)
        uu = jnp.clip(uu, -SWIGLU_LIMIT, SWIGLU_LIMIT)
        glu = gg * (1.0 / (1.0 + jnp.exp(-SWIGLU_ALPHA * gg)))
        act = ((uu + 1.0) * glu).astype(BF16)
        o_ref[...] += _dot(act, wd_ref[0].astype(BF16))

    @pl.when(jnp.logical_and(jnp.logical_not(used), c == 0))
    def _():
        o_ref[...] = jnp.zeros_like(o_ref)


def _experts(block_e, n_used, slot_tok, x1, g, wg, bg, wu, bu, wd, bd):
    n_slots = slot_tok.shape[0]
    d = x1.shape[-1]
    dff = wg.shape[-1]
    n_blocks = n_slots // MOE_BLOCK
    n_chunks = dff // FF_CHUNK
    tok3 = slot_tok.reshape(n_blocks, 1, MOE_BLOCK)
    bg3 = bg.reshape(N_EXPERTS, 1, dff)
    bu3 = bu.reshape(N_EXPERTS, 1, dff)
    bd3 = bd.reshape(N_EXPERTS, 1, d)

    def chunk(i, c, nb):
        return jnp.where(i < nb[0], c, n_chunks - 1)

    grid_spec = pltpu.PrefetchScalarGridSpec(
        num_scalar_prefetch=2,
        grid=(n_blocks, n_chunks),
        in_specs=[
            pl.BlockSpec((1, 1, MOE_BLOCK), lambda i, c, be, nb: (i, 0, 0), memory_space=pltpu.SMEM),
            pl.BlockSpec(memory_space=pl.ANY),
            pl.BlockSpec((1, d), lambda i, c, be, nb: (0, 0)),
            pl.BlockSpec((1, d, FF_CHUNK), lambda i, c, be, nb: (be[i], 0, chunk(i, c, nb))),
            pl.BlockSpec((1, 1, FF_CHUNK), lambda i, c, be, nb: (be[i], 0, chunk(i, c, nb))),
            pl.BlockSpec((1, d, FF_CHUNK), lambda i, c, be, nb: (be[i], 0, chunk(i, c, nb))),
            pl.BlockSpec((1, 1, FF_CHUNK), lambda i, c, be, nb: (be[i], 0, chunk(i, c, nb))),
            pl.BlockSpec((1, FF_CHUNK, d), lambda i, c, be, nb: (be[i], chunk(i, c, nb), 0)),
            pl.BlockSpec((1, 1, d), lambda i, c, be, nb: (be[i], 0, 0)),
        ],
        out_specs=pl.BlockSpec((MOE_BLOCK, d), lambda i, c, be, nb: (i, 0)),
        scratch_shapes=[pltpu.VMEM((MOE_BLOCK, d), F32),
                        pltpu.VMEM((MOE_BLOCK, d), BF16),
                        pltpu.SemaphoreType.DMA(())],
    )
    return pl.pallas_call(
        _experts_body,
        grid_spec=grid_spec,
        out_shape=jax.ShapeDtypeStruct((n_slots, d), F32),
        compiler_params=_params(("arbitrary", "arbitrary")),
        name="experts",
    )(block_e, n_used, tok3, x1, g, wg, bg3, wu, bu3, wd, bd3)


def _combine_body(dest_ref, x1_ref, gate_ref, y_hbm, o_ref, buf_ref, sem):
    tm = x1_ref.shape[0]

    def body(r, carry):
        for kk in range(TOP_K):
            pltpu.make_async_copy(y_hbm.at[pl.ds(dest_ref[0, 0, r * TOP_K + kk], 1), :],
                                  buf_ref.at[kk, pl.ds(r, 1), :], sem).start()
        return carry
    lax.fori_loop(0, tm, body, 0)
    pltpu.make_async_copy(buf_ref, buf_ref, sem).wait()
    acc = x1_ref[...]
    for kk in range(TOP_K):
        acc = acc + gate_ref[:, kk:kk + 1] * buf_ref[kk]
    o_ref[...] = acc


def _combine(dest, x1, gates, yb):
    n, d = x1.shape
    tm = COMB_ROWS
    dest3 = dest.reshape(n // tm, 1, tm * TOP_K)
    row = lambda i: (i, 0)
    return pl.pallas_call(
        _combine_body,
        grid=(n // tm,),
        in_specs=[
            pl.BlockSpec((1, 1, tm * TOP_K), lambda i: (i, 0, 0), memory_space=pltpu.SMEM),
            pl.BlockSpec((tm, d), row),
            pl.BlockSpec((tm, V7X_LANES), row),
            pl.BlockSpec(memory_space=pl.ANY),
        ],
        out_specs=pl.BlockSpec((tm, d), row),
        out_shape=jax.ShapeDtypeStruct((n, d), F32),
        scratch_shapes=[pltpu.VMEM((TOP_K, tm, d), F32), pltpu.SemaphoreType.DMA(())],
        compiler_params=_params(("arbitrary",)),
        name="combine",
    )(dest3, x1, gates, yb)


def _slot_layout(top_idx, n_tok):
    nk = n_tok * TOP_K
    onehot = (top_idx[:, :, None] == jnp.arange(N_EXPERTS, dtype=jnp.int32)[None, None, :])
    member = jnp.sum(onehot.astype(jnp.int32), axis=1)
    csum = jnp.cumsum(member, axis=0)
    counts = csum[-1]
    rank = jnp.take_along_axis(csum - member, top_idx, axis=1)
    padded = ((counts + MOE_BLOCK - 1) // MOE_BLOCK) * MOE_BLOCK
    pend = jnp.cumsum(padded)
    pstart = pend - padded
    dest = (pstart[top_idx] + rank).astype(jnp.int32)
    n_blocks = -(-nk // MOE_BLOCK) + N_EXPERTS
    n_slots = n_blocks * MOE_BLOCK
    tok = jnp.broadcast_to(jnp.arange(n_tok, dtype=jnp.int32)[:, None], dest.shape)
    slot_tok = jnp.zeros((n_slots,), jnp.int32).at[dest.reshape(-1)].set(tok.reshape(-1))
    block_start = jnp.arange(n_blocks, dtype=jnp.int32) * MOE_BLOCK
    block_e = jnp.sum((block_start[:, None] >= pend[None, :]).astype(jnp.int32), axis=1)
    block_e = jnp.minimum(block_e, N_EXPERTS - 1).astype(jnp.int32)
    n_used = (pend[-1] // MOE_BLOCK).astype(jnp.int32).reshape(1)
    return dest, slot_tok, block_e, n_used


def _pad_lanes(v, lane0=0):
    out = jnp.zeros((1, V7X_LANES), F32)
    return out.at[0, lane0:lane0 + v.shape[0]].set(v.astype(F32))


def kernel(x, attn_norm_g, w_in, forget_bias, q_norm_g, k_norm_g, conv_w, conv_b, dt_bias, a_log, d_skip,
           ssd_norm_g, w_out, ffn_norm_g, w_router, b_router, w_gate, b_gate, w_up, b_up, w_down, b_down):
    b, s, d = x.shape
    depth = attn_norm_g.shape[0]
    att_w = ATT_HEADS * ATT_HEAD_DIM
    ssd_w = w_out.shape[1] - att_w
    ssd_heads = ssd_w // SSD_HEAD_DIM
    conv_ch = ssd_w + 2 * SSD_GROUPS * SSD_STATE
    n_tok = b * s
    dt_lane0 = ATT_HEADS
    o_f = 3 * att_w
    o_z = o_f + ATT_HEADS
    o_xbc = o_z + ssd_w
    o_dt = o_xbc + conv_ch

    x2d = x.reshape(n_tok, d)
    for layer in range(depth):
        wl = w_in[layer]
        small_w = jnp.zeros((d, SMALL_COLS), F32)
        small_w = small_w.at[:, 0:ATT_HEADS].set(wl[:, o_f:o_z])
        small_w = small_w.at[:, dt_lane0:dt_lane0 + ssd_heads].set(wl[:, o_dt:o_dt + ssd_heads])
        w_all = jnp.concatenate([wl[:, :o_f], wl[:, o_z:o_dt], small_w], axis=1).astype(BF16)

        q, k, v, z, xbc, small = _in_proj(
            x2d, attn_norm_g[layer][None, :], w_all, q_norm_g[layer][None, :], k_norm_g[layer][None, :],
            att_w=att_w, ssd_w=ssd_w, conv_ch=conv_ch)

        small3 = small.reshape(b, s, SMALL_COLS)
        cum_s, cum_t = _forget_cum(small3, _pad_lanes(forget_bias[layer]))
        att = _attention(q.reshape(b, s, att_w), k.reshape(b, s, att_w), v.reshape(b, s, att_w), cum_s, cum_t)

        ssd = _ssd(xbc.reshape(b, s, conv_ch), z.reshape(b, s, ssd_w), small3,
                   conv_w[layer], conv_b[layer][None, :],
                   _pad_lanes(dt_bias[layer], dt_lane0), _pad_lanes(a_log[layer], dt_lane0),
                   jnp.repeat(d_skip[layer], SSD_HEAD_DIM)[None, :], ssd_norm_g[layer][None, :],
                   dt_lane0=dt_lane0)

        wr = jnp.zeros((d, V7X_LANES), F32).at[:, :N_EXPERTS].set(w_router[layer])
        wr_hi = wr.astype(BF16)
        wr_lo = (wr - wr_hi.astype(F32)).astype(BF16)
        x1, idx_pad, gate_pad = _out_proj(
            att.reshape(n_tok, att_w), ssd.reshape(n_tok, ssd_w), x2d, w_out[layer].astype(BF16),
            ffn_norm_g[layer][None, :], wr_hi, wr_lo, _pad_lanes(b_router[layer]))

        dest, slot_tok, block_e, n_used = _slot_layout(idx_pad[:, :TOP_K], n_tok)
        yb = _experts(block_e, n_used, slot_tok, x1, ffn_norm_g[layer][None, :],
                      w_gate[layer], b_gate[layer], w_up[layer], b_up[layer], w_down[layer], b_down[layer])
        x2d = _combine(dest.reshape(-1), x1, gate_pad, yb)
    return x2d.reshape(b, s, d)
```

```python
import functools
import math

import jax
import jax.numpy as jnp
from jax import lax
from jax.experimental import pallas as pl
from jax.experimental.pallas import tpu as pltpu

F32 = jnp.float32
BF16 = jnp.bfloat16

ATT_HEADS = 8
ATT_HEAD_DIM = 128
SSD_HEAD_DIM = 64
SSD_GROUPS = 2
SSD_STATE = 128
CONV_WIDTH = 4
N_EXPERTS = 32
TOP_K = 4
SWIGLU_ALPHA = 1.702
SWIGLU_LIMIT = 7.0
EPS = 1e-6

V7X_LANES = 128
V7X_SUBLANES = 8
V7X_VMEM_BYTES = 64 * 1024 * 1024
VMEM_LIMIT_BYTES = 58 * 1024 * 1024

PROJ_ROWS = 256
PROJ_COLS = 512
CUM_ROWS = 256
ATT_BLOCK = 512
SSD_ROWS = 256
MOE_BLOCK = 256
GROUP_BLOCKS = 5
ZERO_ROWS = 32
FF_CHUNK = 512
COMB_ROWS = 128
SMALL_COLS = V7X_LANES


def _params(dims):
    return pltpu.CompilerParams(dimension_semantics=dims, vmem_limit_bytes=VMEM_LIMIT_BYTES)


def _rms(x, g, eps):
    return x * lax.rsqrt(jnp.mean(x * x, axis=-1, keepdims=True) + eps) * g


def _split3(x):
    x1 = x.astype(BF16)
    r1 = x - x1.astype(F32)
    x2 = r1.astype(BF16)
    r2 = r1 - x2.astype(F32)
    return x1, x2, r2.astype(BF16)


def _dot(a, b):
    return jnp.dot(a, b, preferred_element_type=F32)


def _tri_cumsum(x, n):
    row = lax.broadcasted_iota(jnp.int32, (n, n), 0)
    col = lax.broadcasted_iota(jnp.int32, (n, n), 1)
    tri = (row >= col).astype(BF16)
    x1, x2, x3 = _split3(x)
    return _dot(tri, x1) + _dot(tri, x2) + _dot(tri, x3)


def _softplus(x):
    return jnp.maximum(x, 0.0) + jnp.log1p(jnp.exp(-jnp.abs(x)))


def _silu(x):
    return x * (1.0 / (1.0 + jnp.exp(-x)))


def _in_proj_body(x_ref, g_ref, w_ref, qg_ref, kg_ref,
                  q_ref, k_ref, v_ref, z_ref, xbc_ref, sm_ref, *, head_dim, q_scale):
    att_w = q_ref.shape[-1]
    ssd_w = z_ref.shape[-1]
    conv_ch = xbc_ref.shape[-1]
    h = _rms(x_ref[...], g_ref[...], EPS).astype(BF16)

    def proj(c0, width):
        return _dot(h, w_ref[:, c0:c0 + width])

    def chunks(width):
        return [(c, min(PROJ_COLS, width - c)) for c in range(0, width, PROJ_COLS)]

    col = 0
    for out_ref, gain_ref, scale in ((q_ref, qg_ref, q_scale), (k_ref, kg_ref, None)):
        for c0, cw in chunks(att_w):
            y = proj(col + c0, cw)
            for j in range(cw // head_dim):
                yn = _rms(y[:, j * head_dim:(j + 1) * head_dim], gain_ref[...], EPS)
                if scale is not None:
                    yn = yn * scale
                out_ref[:, c0 + j * head_dim:c0 + (j + 1) * head_dim] = yn.astype(out_ref.dtype)
        col += att_w
    for out_ref, width in ((v_ref, att_w), (z_ref, ssd_w), (xbc_ref, conv_ch), (sm_ref, SMALL_COLS)):
        for c0, cw in chunks(width):
            out_ref[:, c0:c0 + cw] = proj(col + c0, cw).astype(out_ref.dtype)
        col += width


def _in_proj(x2d, g, w_all, qg, kg, *, att_w, ssd_w, conv_ch):
    n, d = x2d.shape
    tm = PROJ_ROWS
    row = lambda i: (i, 0)
    fixed = lambda i: (0, 0)
    outs = [(att_w, BF16), (att_w, BF16), (att_w, BF16), (ssd_w, F32), (conv_ch, F32), (SMALL_COLS, F32)]
    return pl.pallas_call(
        functools.partial(_in_proj_body, head_dim=ATT_HEAD_DIM, q_scale=1.0 / math.sqrt(ATT_HEAD_DIM)),
        grid=(n // tm,),
        in_specs=[
            pl.BlockSpec((tm, d), row),
            pl.BlockSpec((1, d), fixed),
            pl.BlockSpec(w_all.shape, fixed, pipeline_mode=pl.Buffered(1)),
            pl.BlockSpec((1, ATT_HEAD_DIM), fixed),
            pl.BlockSpec((1, ATT_HEAD_DIM), fixed),
        ],
        out_specs=[pl.BlockSpec((tm, w), row) for w, _ in outs],
        out_shape=[jax.ShapeDtypeStruct((n, w), dt) for w, dt in outs],
        compiler_params=_params(("arbitrary",)),
        name="in_proj",
    )(x2d, g, w_all, qg, kg)


def _forget_cum_body(sm_ref, fb_ref, cs_ref, ct_ref, carry_ref):
    @pl.when(pl.program_id(1) == 0)
    def _():
        carry_ref[...] = jnp.zeros_like(carry_ref)

    n = sm_ref.shape[1]
    log_f = -_softplus(-(sm_ref[0] + fb_ref[...]))
    c = _tri_cumsum(log_f, n) + carry_ref[...]
    cs_ref[0] = c
    ct_ref[0] = c.T[0:ct_ref.shape[1], :]
    carry_ref[...] = c[n - 1:n, :]


def _forget_cum(small, fb):
    b, s, w = small.shape
    tc = CUM_ROWS
    return pl.pallas_call(
        _forget_cum_body,
        grid=(b, s // tc),
        in_specs=[pl.BlockSpec((1, tc, w), lambda i, j: (i, j, 0)),
                  pl.BlockSpec((1, w), lambda i, j: (0, 0))],
        out_specs=[pl.BlockSpec((1, tc, w), lambda i, j: (i, j, 0)),
                   pl.BlockSpec((1, V7X_SUBLANES, tc), lambda i, j: (i, 0, j))],
        out_shape=[jax.ShapeDtypeStruct((b, s, w), F32),
                   jax.ShapeDtypeStruct((b, V7X_SUBLANES, s), F32)],
        scratch_shapes=[pltpu.VMEM((1, w), F32)],
        compiler_params=_params(("arbitrary", "arbitrary")),
        name="forget_cum",
    )(small, fb)


def _attn_body(qi_ref, ki_ref, q_ref, k_ref, v_ref, cs_ref, ct_ref, o_ref,
               m_ref, l_ref, acc_ref, cq_ref):
    head = pl.program_id(1)
    p = pl.program_id(2)
    qi = qi_ref[p]
    ki = ki_ref[p]
    tq = q_ref.shape[1]
    tk = k_ref.shape[1]

    @pl.when(ki == 0)
    def _():
        m_ref[...] = jnp.full_like(m_ref, -jnp.inf)
        l_ref[...] = jnp.zeros_like(l_ref)
        acc_ref[...] = jnp.zeros_like(acc_ref)
        lane = lax.broadcasted_iota(jnp.int32, cs_ref.shape[1:], 1)
        cq_ref[...] = jnp.sum(jnp.where(lane == head, cs_ref[0], 0.0), axis=-1, keepdims=True)

    s = lax.dot_general(q_ref[0], k_ref[0], (((1,), (1,)), ((), ())), preferred_element_type=F32)
    ck = ct_ref[0, pl.ds(head, 1), :]
    s = (s + cq_ref[...]) - ck
    q_pos = qi * tq + lax.broadcasted_iota(jnp.int32, (tq, tk), 0)
    k_pos = ki * tk + lax.broadcasted_iota(jnp.int32, (tq, tk), 1)
    s = jnp.where(k_pos <= q_pos, s, -jnp.inf)

    m_old = m_ref[...]
    m_new = jnp.maximum(m_old, jnp.max(s, axis=-1, keepdims=True))
    alpha = jnp.exp(m_old - m_new)
    pmat = jnp.exp(s - m_new)
    l_ref[...] = alpha * l_ref[...] + jnp.sum(pmat, axis=-1, keepdims=True)
    acc_ref[...] = alpha * acc_ref[...] + _dot(pmat.astype(BF16), v_ref[0])
    m_ref[...] = m_new

    @pl.when(ki == qi)
    def _():
        o_ref[0] = (acc_ref[...] / l_ref[...]).astype(o_ref.dtype)


def _attention(q, k, v, cum_s, cum_t):
    b, s, w = q.shape
    hd = ATT_HEAD_DIM
    heads = w // hd
    t = ATT_BLOCK
    nq = s // t
    pairs = [(i, j) for i in range(nq) for j in range(i + 1)]
    qi = jnp.asarray([p[0] for p in pairs], jnp.int32)
    ki = jnp.asarray([p[1] for p in pairs], jnp.int32)
    grid_spec = pltpu.PrefetchScalarGridSpec(
        num_scalar_prefetch=2,
        grid=(b, heads, len(pairs)),
        in_specs=[
            pl.BlockSpec((1, t, hd), lambda bi, h, p, qi, ki: (bi, qi[p], h)),
            pl.BlockSpec((1, t, hd), lambda bi, h, p, qi, ki: (bi, ki[p], h)),
            pl.BlockSpec((1, t, hd), lambda bi, h, p, qi, ki: (bi, ki[p], h)),
            pl.BlockSpec((1, t, cum_s.shape[-1]), lambda bi, h, p, qi, ki: (bi, qi[p], 0)),
            pl.BlockSpec((1, cum_t.shape[1], t), lambda bi, h, p, qi, ki: (bi, 0, ki[p])),
        ],
        out_specs=pl.BlockSpec((1, t, hd), lambda bi, h, p, qi, ki: (bi, qi[p], h)),
        scratch_shapes=[pltpu.VMEM((t, 1), F32), pltpu.VMEM((t, 1), F32),
                        pltpu.VMEM((t, hd), F32), pltpu.VMEM((t, 1), F32)],
    )
    return pl.pallas_call(
        _attn_body,
        grid_spec=grid_spec,
        out_shape=jax.ShapeDtypeStruct((b, s, w), BF16),
        compiler_params=_params(("arbitrary", "arbitrary", "arbitrary")),
        name="attention",
    )(qi, ki, q, k, v, cum_s, cum_t)


def _ssd_body(xbc_ref, z_ref, sm_ref, cw_ref, cb_ref, dtb_ref, alog_ref, dskip_ref, ng_ref,
              o_ref, ext_ref, state_ref, y_ref, *, dt_lane0):
    t = xbc_ref.shape[1]
    width = z_ref.shape[-1]
    lanes = V7X_LANES
    n_pairs = width // lanes
    pairs_per_group = n_pairs // SSD_GROUPS
    tail = V7X_SUBLANES

    @pl.when(pl.program_id(1) == 0)
    def _():
        ext_ref[0:tail, :] = jnp.zeros((tail, ext_ref.shape[1]), F32)
        state_ref[...] = jnp.zeros_like(state_ref)

    ext_ref[tail:tail + t, :] = xbc_ref[0]
    conv = cb_ref[...]
    for tap in range(CONV_WIDTH):
        off = tail - (CONV_WIDTH - 1) + tap
        conv = conv + ext_ref[off:off + t, :] * cw_ref[tap:tap + 1, :]
    ext_ref[0:tail, :] = ext_ref[t:t + tail, :]
    conv = _silu(conv)

    sm = sm_ref[0]
    dt = _softplus(sm + dtb_ref[...])
    d_a = dt * (-jnp.exp(alog_ref[...]))
    a_cs = _tri_cumsum(d_a, t)
    a_cs_t = a_cs.T
    a_last = a_cs[t - 1:t, :]

    row = lax.broadcasted_iota(jnp.int32, (t, t), 0)
    col = lax.broadcasted_iota(jnp.int32, (t, t), 1)
    causal = row >= col
    lane = lax.broadcasted_iota(jnp.int32, (t, lanes), 1)
    first_half = lane < SSD_HEAD_DIM
    lane1 = lax.broadcasted_iota(jnp.int32, (1, lanes), 1)

    b_off = width
    c_off = width + SSD_GROUPS * SSD_STATE
    sumsq = jnp.zeros((t, 1), F32)
    for g in range(SSD_GROUPS):
        b_g = conv[:, b_off + g * SSD_STATE:b_off + (g + 1) * SSD_STATE].astype(BF16)
        c_g = conv[:, c_off + g * SSD_STATE:c_off + (g + 1) * SSD_STATE].astype(BF16)
        cb = lax.dot_general(c_g, b_g, (((1,), (1,)), ((), ())), preferred_element_type=F32)
        for jp in range(pairs_per_group):
            pr = g * pairs_per_group + jp
            h0 = dt_lane0 + 2 * pr
            xs = conv[:, pr * lanes:(pr + 1) * lanes]
            a_pair = jnp.where(first_half, a_cs[:, h0:h0 + 1], a_cs[:, h0 + 1:h0 + 2])
            dt_pair = jnp.where(first_half, dt[:, h0:h0 + 1], dt[:, h0 + 1:h0 + 2])
            a_last_pair = jnp.where(lane1 < SSD_HEAD_DIM, a_last[:, h0:h0 + 1], a_last[:, h0 + 1:h0 + 2])
            x_dt = xs * dt_pair
            y = jnp.zeros((t, lanes), F32)
            for half in range(2):
                hh = h0 + half
                diff = a_cs[:, hh:hh + 1] - a_cs_t[hh:hh + 1, :]
                wmat = cb * jnp.exp(jnp.where(causal, diff, -jnp.inf))
                keep = first_half if half == 0 else jnp.logical_not(first_half)
                y = y + _dot(wmat.astype(BF16), jnp.where(keep, x_dt, 0.0).astype(BF16))
            st = state_ref[pr]
            y = y + _dot(c_g, st.astype(BF16)) * jnp.exp(a_pair)
            x_dec = (x_dt * jnp.exp(a_last_pair - a_pair)).astype(BF16)
            upd = lax.dot_general(b_g, x_dec, (((0,), (0,)), ((), ())), preferred_element_type=F32)
            state_ref[pr] = st * jnp.exp(a_last_pair) + upd
            y = y + xs * dskip_ref[:, pr * lanes:(pr + 1) * lanes]
            zz = z_ref[0, :, pr * lanes:(pr + 1) * lanes]
            y = y * _silu(zz)
            sumsq = sumsq + jnp.sum(y * y, axis=-1, keepdims=True)
            y_ref[:, pr * lanes:(pr + 1) * lanes] = y
    scale = lax.rsqrt(sumsq / width + EPS)
    o_ref[0] = (y_ref[...] * scale * ng_ref[...]).astype(o_ref.dtype)


def _ssd(xbc, z, small, conv_w, conv_b, dtb, alog, dskip, norm_g, *, dt_lane0):
    b, s, cc = xbc.shape
    width = z.shape[-1]
    t = SSD_ROWS
    n_pairs = width // V7X_LANES
    blk = lambda i, j: (i, j, 0)
    fixed = lambda i, j: (0, 0)
    return pl.pallas_call(
        functools.partial(_ssd_body, dt_lane0=dt_lane0),
        grid=(b, s // t),
        in_specs=[
            pl.BlockSpec((1, t, cc), blk),
            pl.BlockSpec((1, t, width), blk),
            pl.BlockSpec((1, t, small.shape[-1]), blk),
            pl.BlockSpec(conv_w.shape, fixed),
            pl.BlockSpec(conv_b.shape, fixed),
            pl.BlockSpec(dtb.shape, fixed),
            pl.BlockSpec(alog.shape, fixed),
            pl.BlockSpec(dskip.shape, fixed),
            pl.BlockSpec(norm_g.shape, fixed),
        ],
        out_specs=pl.BlockSpec((1, t, width), blk),
        out_shape=jax.ShapeDtypeStruct((b, s, width), BF16),
        scratch_shapes=[
            pltpu.VMEM((t + V7X_SUBLANES, cc), F32),
            pltpu.VMEM((n_pairs, SSD_STATE, V7X_LANES), F32),
            pltpu.VMEM((t, width), F32),
        ],
        compiler_params=_params(("arbitrary", "arbitrary")),
        name="ssd",
    )(xbc, z, small, conv_w, conv_b, dtb, alog, dskip, norm_g)


def _out_proj_body(att_ref, ssd_ref, x_ref, wo_ref, g_ref, wrh_ref, wrl_ref, br_ref,
                   x1_ref, idx_ref, gate_ref):
    aw = att_ref.shape[-1]
    d = x_ref.shape[-1]
    tm = x_ref.shape[0]
    for c0 in range(0, d, PROJ_COLS):
        y = _dot(att_ref[...], wo_ref[0:aw, c0:c0 + PROJ_COLS])
        y = y + _dot(ssd_ref[...], wo_ref[aw:, c0:c0 + PROJ_COLS])
        x1_ref[:, c0:c0 + PROJ_COLS] = x_ref[:, c0:c0 + PROJ_COLS] + y
    h2 = _rms(x1_ref[...], g_ref[...], EPS)
    h_hi = h2.astype(BF16)
    h_lo = (h2 - h_hi.astype(F32)).astype(BF16)
    logits = _dot(h_hi, wrh_ref[...]) + _dot(h_lo, wrh_ref[...]) + _dot(h_hi, wrl_ref[...]) + br_ref[...]
    lane = lax.broadcasted_iota(jnp.int32, (tm, V7X_LANES), 1)
    logits = jnp.where(lane < N_EXPERTS, logits, -jnp.inf)
    lane_f = lane.astype(F32)
    idx_out = jnp.zeros((tm, V7X_LANES), F32)
    vals = []
    for kk in range(TOP_K):
        m = jnp.max(logits, axis=-1, keepdims=True)
        sel = jnp.min(jnp.where(logits == m, lane_f, float(V7X_LANES)), axis=-1, keepdims=True)
        idx_out = jnp.where(lane == kk, sel, idx_out)
        logits = jnp.where(lane_f == sel, -jnp.inf, logits)
        vals.append(m)
    exps = [jnp.exp(v - vals[0]) for v in vals]
    denom = exps[0]
    for e in exps[1:]:
        denom = denom + e
    gate_out = jnp.zeros((tm, V7X_LANES), F32)
    for kk in range(TOP_K):
        gate_out = jnp.where(lane == kk, exps[kk] / denom, gate_out)
    idx_ref[...] = idx_out.astype(jnp.int32)
    gate_ref[...] = gate_out


def _out_proj(att, ssd, x2d, wo, g, wr_hi, wr_lo, br):
    n, d = x2d.shape
    tm = PROJ_ROWS
    row = lambda i: (i, 0)
    fixed = lambda i: (0, 0)
    return pl.pallas_call(
        _out_proj_body,
        grid=(n // tm,),
        in_specs=[
            pl.BlockSpec((tm, att.shape[-1]), row),
            pl.BlockSpec((tm, ssd.shape[-1]), row),
            pl.BlockSpec((tm, d), row),
            pl.BlockSpec(wo.shape, fixed, pipeline_mode=pl.Buffered(1)),
            pl.BlockSpec((1, d), fixed),
            pl.BlockSpec(wr_hi.shape, fixed),
            pl.BlockSpec(wr_lo.shape, fixed),
            pl.BlockSpec((1, V7X_LANES), fixed),
        ],
        out_specs=[pl.BlockSpec((tm, d), row),
                   pl.BlockSpec((tm, V7X_LANES), row),
                   pl.BlockSpec((tm, V7X_LANES), row)],
        out_shape=[jax.ShapeDtypeStruct((n, d), F32),
                   jax.ShapeDtypeStruct((n, V7X_LANES), jnp.int32),
                   jax.ShapeDtypeStruct((n, V7X_LANES), F32)],
        compiler_params=_params(("arbitrary",)),
        name="out_proj",
    )(att, ssd, x2d, wo, g, wr_hi, wr_lo, br)


def _experts_body(ge_ref, gb_ref, gn_ref, nu_ref, tok_hbm, x1_hbm, g_ref, wg_ref, bg_ref, wu_ref, bu_ref, wd_ref,
                  bd_ref, y_hbm, tok_ref, xraw_ref, xs_ref, acc_ref, zero_ref, tok_sem, in_sems, out_sem, zero_sem):
    grp = pl.program_id(0)
    c = pl.program_id(1)
    n_chunks = pl.num_programs(1)
    nblk = gn_ref[grp]
    blk0 = gb_ref[grp]
    rows = MOE_BLOCK
    n_blocks = y_hbm.shape[0] // rows
    zrows = zero_ref.shape[0]

    def block_rows(j):
        return pl.ds(pl.multiple_of(j * rows, rows), rows)

    def zero_fill(wait):
        def body(blk, carry):
            for q in range(rows // zrows):
                dst = pl.ds(pl.multiple_of(blk * rows + q * zrows, zrows), zrows)
                cp = pltpu.make_async_copy(zero_ref, y_hbm.at[dst, :], zero_sem)
                if wait:
                    cp.wait()
                else:
                    cp.start()
            return carry
        lax.fori_loop(nu_ref[0], n_blocks, body, 0)

    @pl.when(jnp.logical_and(grp == 0, c == 0))
    def _():
        zero_ref[...] = jnp.zeros_like(zero_ref)
        zero_fill(wait=False)

    def start_gather(j):
        slot = j % 2

        def body(r, carry):
            pltpu.make_async_copy(x1_hbm.at[pl.ds(tok_ref[j, 0, r], 1), :],
                                  xraw_ref.at[slot, pl.ds(r, 1), :], in_sems.at[slot]).start()
            return carry
        lax.fori_loop(0, rows, body, 0, unroll=8)

    @pl.when(jnp.logical_and(nblk > 0, c == 0))
    def _():
        ids = pltpu.make_async_copy(tok_hbm.at[pl.ds(blk0, GROUP_BLOCKS)], tok_ref, tok_sem)
        ids.start()
        ids.wait()
        start_gather(0)

        def body(j, carry):
            slot = j % 2

            @pl.when(j + 1 < nblk)
            def _():
                start_gather(j + 1)
            pltpu.make_async_copy(xraw_ref.at[slot], xraw_ref.at[slot], in_sems.at[slot]).wait()
            xs_ref[block_rows(j), :] = _rms(xraw_ref[slot], g_ref[...], EPS).astype(BF16)
            acc_ref[block_rows(j), :] = jnp.broadcast_to(bd_ref[0], (rows, acc_ref.shape[1]))
            return carry
        lax.fori_loop(0, nblk, body, 0)

    @pl.when(nblk > 0)
    def _():
        wg = wg_ref[0].astype(BF16)
        wu = wu_ref[0].astype(BF16)
        wd = wd_ref[0].astype(BF16)

        def body(j, carry):
            xs = xs_ref[block_rows(j), :]
            gg = _dot(xs, wg) + bg_ref[0]
            uu = _dot(xs, wu) + bu_ref[0]
            gg = jnp.minimum(gg, SWIGLU_LIMIT)
            uu = jnp.clip(uu, -SWIGLU_LIMIT, SWIGLU_LIMIT)
            glu = gg * (1.0 / (1.0 + jnp.exp(-SWIGLU_ALPHA * gg)))
            act = ((uu + 1.0) * glu).astype(BF16)
            acc_ref[block_rows(j), :] += _dot(act, wd)
            return carry
        lax.fori_loop(0, nblk, body, 0)

    @pl.when(jnp.logical_and(nblk > 0, c == n_chunks - 1))
    def _():
        def out_copy(j):
            dst = pl.ds(pl.multiple_of((blk0 + j) * rows, rows), rows)
            return pltpu.make_async_copy(acc_ref.at[block_rows(j), :], y_hbm.at[dst, :], out_sem)

        def start(j, carry):
            out_copy(j).start()
            return carry

        def wait(j, carry):
            out_copy(j).wait()
            return carry
        lax.fori_loop(0, nblk, start, 0)
        lax.fori_loop(0, nblk, wait, 0)

    @pl.when(jnp.logical_and(grp == pl.num_programs(0) - 1, c == n_chunks - 1))
    def _():
        zero_fill(wait=True)


def _experts(group_e, group_blk, group_n, n_used, slot_tok, x1, g, wg, bg, wu, bu, wd, bd):
    n_slots = slot_tok.shape[0]
    d = x1.shape[-1]
    dff = wg.shape[-1]
    n_blocks = n_slots // MOE_BLOCK
    n_chunks = dff // FF_CHUNK
    n_groups = group_e.shape[0]
    tok2 = jnp.concatenate([slot_tok.reshape(n_blocks, 1, MOE_BLOCK),
                            jnp.zeros((GROUP_BLOCKS, 1, MOE_BLOCK), jnp.int32)], axis=0)
    bg3 = bg.reshape(N_EXPERTS, 1, dff)
    bu3 = bu.reshape(N_EXPERTS, 1, dff)
    bd3 = bd.reshape(N_EXPERTS, 1, d)

    def chunk(i, c, gn):
        return jnp.where(gn[i] > 0, c, n_chunks - 1)

    grid_spec = pltpu.PrefetchScalarGridSpec(
        num_scalar_prefetch=4,
        grid=(n_groups, n_chunks),
        in_specs=[
            pl.BlockSpec(memory_space=pl.ANY),
            pl.BlockSpec(memory_space=pl.ANY),
            pl.BlockSpec((1, d), lambda i, c, ge, gb, gn, nu: (0, 0)),
            pl.BlockSpec((1, d, FF_CHUNK), lambda i, c, ge, gb, gn, nu: (ge[i], 0, chunk(i, c, gn))),
            pl.BlockSpec((1, 1, FF_CHUNK), lambda i, c, ge, gb, gn, nu: (ge[i], 0, chunk(i, c, gn))),
            pl.BlockSpec((1, d, FF_CHUNK), lambda i, c, ge, gb, gn, nu: (ge[i], 0, chunk(i, c, gn))),
            pl.BlockSpec((1, 1, FF_CHUNK), lambda i, c, ge, gb, gn, nu: (ge[i], 0, chunk(i, c, gn))),
            pl.BlockSpec((1, FF_CHUNK, d), lambda i, c, ge, gb, gn, nu: (ge[i], chunk(i, c, gn), 0)),
            pl.BlockSpec((1, 1, d), lambda i, c, ge, gb, gn, nu: (ge[i], 0, 0)),
        ],
        out_specs=pl.BlockSpec(memory_space=pl.ANY),
        scratch_shapes=[pltpu.SMEM((GROUP_BLOCKS, 1, MOE_BLOCK), jnp.int32),
                        pltpu.VMEM((2, MOE_BLOCK, d), F32),
                        pltpu.VMEM((GROUP_BLOCKS * MOE_BLOCK, d), BF16),
                        pltpu.VMEM((GROUP_BLOCKS * MOE_BLOCK, d), F32),
                        pltpu.VMEM((ZERO_ROWS, d), F32),
                        pltpu.SemaphoreType.DMA(()),
                        pltpu.SemaphoreType.DMA((2,)),
                        pltpu.SemaphoreType.DMA(()),
                        pltpu.SemaphoreType.DMA(())],
    )
    return pl.pallas_call(
        _experts_body,
        grid_spec=grid_spec,
        out_shape=jax.ShapeDtypeStruct((n_slots, d), F32),
        compiler_params=_params(("arbitrary", "arbitrary")),
        name="experts",
    )(group_e, group_blk, group_n, n_used, tok2, x1, g, wg, bg3, wu, bu3, wd, bd3)


def _combine_body(dest_ref, dest_next_ref, x1_ref, gate_ref, y_hbm, o_ref, buf_ref, sems):
    tm = x1_ref.shape[0]
    i = pl.program_id(0)
    slot = i % 2

    def start_gather(ids_ref, s):
        def body(r, carry):
            for kk in range(TOP_K):
                pltpu.make_async_copy(y_hbm.at[pl.ds(ids_ref[0, 0, r * TOP_K + kk], 1), :],
                                      buf_ref.at[s, kk, pl.ds(r, 1), :], sems.at[s]).start()
            return carry
        lax.fori_loop(0, tm, body, 0, unroll=4)

    @pl.when(i == 0)
    def _():
        start_gather(dest_ref, 0)

    @pl.when(i + 1 < pl.num_programs(0))
    def _():
        start_gather(dest_next_ref, 1 - slot)

    pltpu.make_async_copy(buf_ref.at[slot], buf_ref.at[slot], sems.at[slot]).wait()
    acc = x1_ref[...]
    for kk in range(TOP_K):
        acc = acc + gate_ref[:, kk:kk + 1] * buf_ref[slot, kk]
    o_ref[...] = acc


def _combine(dest, x1, gates, yb):
    n, d = x1.shape
    tm = COMB_ROWS
    n_tiles = n // tm
    dest3 = dest.reshape(n_tiles, 1, tm * TOP_K)
    row = lambda i: (i, 0)
    return pl.pallas_call(
        _combine_body,
        grid=(n_tiles,),
        in_specs=[
            pl.BlockSpec((1, 1, tm * TOP_K), lambda i: (i, 0, 0), memory_space=pltpu.SMEM),
            pl.BlockSpec((1, 1, tm * TOP_K), lambda i: (jnp.minimum(i + 1, n_tiles - 1), 0, 0),
                         memory_space=pltpu.SMEM),
            pl.BlockSpec((tm, d), row),
            pl.BlockSpec((tm, V7X_LANES), row),
            pl.BlockSpec(memory_space=pl.ANY),
        ],
        out_specs=pl.BlockSpec((tm, d), row),
        out_shape=jax.ShapeDtypeStruct((n, d), F32),
        scratch_shapes=[pltpu.VMEM((2, TOP_K, tm, d), F32), pltpu.SemaphoreType.DMA((2,))],
        compiler_params=_params(("arbitrary",)),
        name="combine",
    )(dest3, dest3, x1, gates, yb)


def _slot_layout(top_idx, n_tok):
    nk = n_tok * TOP_K
    onehot = (top_idx[:, :, None] == jnp.arange(N_EXPERTS, dtype=jnp.int32)[None, None, :])
    member = jnp.sum(onehot.astype(jnp.int32), axis=1)
    csum = jnp.cumsum(member, axis=0)
    counts = csum[-1]
    rank = jnp.take_along_axis(csum - member, top_idx, axis=1)
    padded = ((counts + MOE_BLOCK - 1) // MOE_BLOCK) * MOE_BLOCK
    pend = jnp.cumsum(padded)
    pstart = pend - padded
    dest = (pstart[top_idx] + rank).astype(jnp.int32)
    n_blocks = -(-nk // MOE_BLOCK) + N_EXPERTS
    n_slots = n_blocks * MOE_BLOCK
    tok = jnp.broadcast_to(jnp.arange(n_tok, dtype=jnp.int32)[:, None], dest.shape)
    slot_tok = jnp.zeros((n_slots,), jnp.int32).at[dest.reshape(-1)].set(tok.reshape(-1))
    eblk = (padded // MOE_BLOCK).astype(jnp.int32)
    gcount = (eblk + GROUP_BLOCKS - 1) // GROUP_BLOCKS
    gend = jnp.cumsum(gcount)
    gstart = gend - gcount
    n_groups = (n_blocks + N_EXPERTS * (GROUP_BLOCKS - 1)) // GROUP_BLOCKS
    gid = jnp.arange(n_groups, dtype=jnp.int32)
    valid = gid < gend[-1]
    last_gid = jnp.maximum(gend[-1] - 1, 0)
    gsel = jnp.where(valid, gid, last_gid)
    group_e = jnp.minimum(jnp.sum((gsel[:, None] >= gend[None, :]).astype(jnp.int32), axis=1), N_EXPERTS - 1)
    local = gsel - gstart[group_e]
    group_blk = (pstart[group_e] // MOE_BLOCK + local * GROUP_BLOCKS).astype(jnp.int32)
    group_n = jnp.where(valid, jnp.clip(eblk[group_e] - local * GROUP_BLOCKS, 0, GROUP_BLOCKS), 0).astype(jnp.int32)
    n_used = (pend[-1] // MOE_BLOCK).astype(jnp.int32).reshape(1)
    return dest, slot_tok, group_e.astype(jnp.int32), group_blk, group_n, n_used


def _pad_lanes(v, lane0=0):
    out = jnp.zeros((1, V7X_LANES), F32)
    return out.at[0, lane0:lane0 + v.shape[0]].set(v.astype(F32))


def kernel(x, attn_norm_g, w_in, forget_bias, q_norm_g, k_norm_g, conv_w, conv_b, dt_bias, a_log, d_skip,
           ssd_norm_g, w_out, ffn_norm_g, w_router, b_router, w_gate, b_gate, w_up, b_up, w_down, b_down):
    b, s, d = x.shape
    depth = attn_norm_g.shape[0]
    att_w = ATT_HEADS * ATT_HEAD_DIM
    ssd_w = w_out.shape[1] - att_w
    ssd_heads = ssd_w // SSD_HEAD_DIM
    conv_ch = ssd_w + 2 * SSD_GROUPS * SSD_STATE
    n_tok = b * s
    dt_lane0 = ATT_HEADS
    o_f = 3 * att_w
    o_z = o_f + ATT_HEADS
    o_xbc = o_z + ssd_w
    o_dt = o_xbc + conv_ch

    x2d = x.reshape(n_tok, d)
    for layer in range(depth):
        wl = w_in[layer]
        small_w = jnp.zeros((d, SMALL_COLS), F32)
        small_w = small_w.at[:, 0:ATT_HEADS].set(wl[:, o_f:o_z])
        small_w = small_w.at[:, dt_lane0:dt_lane0 + ssd_heads].set(wl[:, o_dt:o_dt + ssd_heads])
        w_all = jnp.concatenate([wl[:, :o_f], wl[:, o_z:o_dt], small_w], axis=1).astype(BF16)

        q, k, v, z, xbc, small = _in_proj(
            x2d, attn_norm_g[layer][None, :], w_all, q_norm_g[layer][None, :], k_norm_g[layer][None, :],
            att_w=att_w, ssd_w=ssd_w, conv_ch=conv_ch)

        small3 = small.reshape(b, s, SMALL_COLS)
        cum_s, cum_t = _forget_cum(small3, _pad_lanes(forget_bias[layer]))
        att = _attention(q.reshape(b, s, att_w), k.reshape(b, s, att_w), v.reshape(b, s, att_w), cum_s, cum_t)

        ssd = _ssd(xbc.reshape(b, s, conv_ch), z.reshape(b, s, ssd_w), small3,
                   conv_w[layer], conv_b[layer][None, :],
                   _pad_lanes(dt_bias[layer], dt_lane0), _pad_lanes(a_log[layer], dt_lane0),
                   jnp.repeat(d_skip[layer], SSD_HEAD_DIM)[None, :], ssd_norm_g[layer][None, :],
                   dt_lane0=dt_lane0)

        wr = jnp.zeros((d, V7X_LANES), F32).at[:, :N_EXPERTS].set(w_router[layer])
        wr_hi = wr.astype(BF16)
        wr_lo = (wr - wr_hi.astype(F32)).astype(BF16)
        x1, idx_pad, gate_pad = _out_proj(
            att.reshape(n_tok, att_w), ssd.reshape(n_tok, ssd_w), x2d, w_out[layer].astype(BF16),
            ffn_norm_g[layer][None, :], wr_hi, wr_lo, _pad_lanes(b_router[layer]))

        dest, slot_tok, group_e, group_blk, group_n, n_used = _slot_layout(idx_pad[:, :TOP_K], n_tok)
        yb = _experts(group_e, group_blk, group_n, n_used, slot_tok, x1, ffn_norm_g[layer][None, :],
                      w_gate[layer], b_gate[layer], w_up[layer], b_up[layer], w_down[layer], b_down[layer])
        x2d = _combine(dest.reshape(-1), x1, gate_pad, yb)
    return x2d.reshape(b, s, d)
```

```python
import functools
import math

import jax
import jax.numpy as jnp
from jax import lax
from jax.experimental import pallas as pl
from jax.experimental.pallas import tpu as pltpu

F32 = jnp.float32
BF16 = jnp.bfloat16

ATT_HEADS = 8
ATT_HEAD_DIM = 128
SSD_HEAD_DIM = 64
SSD_GROUPS = 2
SSD_STATE = 128
CONV_WIDTH = 4
N_EXPERTS = 32
TOP_K = 4
SWIGLU_ALPHA = 1.702
SWIGLU_LIMIT = 7.0
EPS = 1e-6

V7X_LANES = 128
V7X_SUBLANES = 8
V7X_VMEM_BYTES = 64 * 1024 * 1024
VMEM_LIMIT_BYTES = 58 * 1024 * 1024

PROJ_ROWS = 256
PROJ_COLS = 512
CUM_ROWS = 256
ATT_BLOCK = 512
SSD_ROWS = 256
DISP_ROWS = 256
MOE_BLOCK = 256
GROUP_BLOCKS = 5
ZERO_ROWS = 32
FF_CHUNK = 512
COMB_ROWS = 128
SMALL_COLS = V7X_LANES


def _params(dims):
    return pltpu.CompilerParams(dimension_semantics=dims, vmem_limit_bytes=VMEM_LIMIT_BYTES)


def _rms(x, g, eps):
    return x * lax.rsqrt(jnp.mean(x * x, axis=-1, keepdims=True) + eps) * g


def _split3(x):
    x1 = x.astype(BF16)
    r1 = x - x1.astype(F32)
    x2 = r1.astype(BF16)
    r2 = r1 - x2.astype(F32)
    return x1, x2, r2.astype(BF16)


def _dot(a, b):
    return jnp.dot(a, b, preferred_element_type=F32)


def _tri_cumsum(x, n):
    row = lax.broadcasted_iota(jnp.int32, (n, n), 0)
    col = lax.broadcasted_iota(jnp.int32, (n, n), 1)
    tri = (row >= col).astype(BF16)
    x1, x2, x3 = _split3(x)
    return _dot(tri, x1) + _dot(tri, x2) + _dot(tri, x3)


def _softplus(x):
    return jnp.maximum(x, 0.0) + jnp.log1p(jnp.exp(-jnp.abs(x)))


def _silu(x):
    return x * (1.0 / (1.0 + jnp.exp(-x)))


def _in_proj_body(x_ref, g_ref, w_ref, qg_ref, kg_ref,
                  q_ref, k_ref, v_ref, z_ref, xbc_ref, sm_ref, *, head_dim, q_scale):
    att_w = q_ref.shape[-1]
    ssd_w = z_ref.shape[-1]
    conv_ch = xbc_ref.shape[-1]
    h = _rms(x_ref[...], g_ref[...], EPS).astype(BF16)

    def proj(c0, width):
        return _dot(h, w_ref[:, c0:c0 + width])

    def chunks(width):
        return [(c, min(PROJ_COLS, width - c)) for c in range(0, width, PROJ_COLS)]

    col = 0
    for out_ref, gain_ref, scale in ((q_ref, qg_ref, q_scale), (k_ref, kg_ref, None)):
        for c0, cw in chunks(att_w):
            y = proj(col + c0, cw)
            for j in range(cw // head_dim):
                yn = _rms(y[:, j * head_dim:(j + 1) * head_dim], gain_ref[...], EPS)
                if scale is not None:
                    yn = yn * scale
                out_ref[:, c0 + j * head_dim:c0 + (j + 1) * head_dim] = yn.astype(out_ref.dtype)
        col += att_w
    for out_ref, width in ((v_ref, att_w), (z_ref, ssd_w), (xbc_ref, conv_ch), (sm_ref, SMALL_COLS)):
        for c0, cw in chunks(width):
            out_ref[:, c0:c0 + cw] = proj(col + c0, cw).astype(out_ref.dtype)
        col += width


def _in_proj(x2d, g, w_all, qg, kg, *, att_w, ssd_w, conv_ch):
    n, d = x2d.shape
    tm = PROJ_ROWS
    row = lambda i: (i, 0)
    fixed = lambda i: (0, 0)
    outs = [(att_w, BF16), (att_w, BF16), (att_w, BF16), (ssd_w, F32), (conv_ch, F32), (SMALL_COLS, F32)]
    return pl.pallas_call(
        functools.partial(_in_proj_body, head_dim=ATT_HEAD_DIM, q_scale=1.0 / math.sqrt(ATT_HEAD_DIM)),
        grid=(n // tm,),
        in_specs=[
            pl.BlockSpec((tm, d), row),
            pl.BlockSpec((1, d), fixed),
            pl.BlockSpec(w_all.shape, fixed, pipeline_mode=pl.Buffered(1)),
            pl.BlockSpec((1, ATT_HEAD_DIM), fixed),
            pl.BlockSpec((1, ATT_HEAD_DIM), fixed),
        ],
        out_specs=[pl.BlockSpec((tm, w), row) for w, _ in outs],
        out_shape=[jax.ShapeDtypeStruct((n, w), dt) for w, dt in outs],
        compiler_params=_params(("arbitrary",)),
        name="in_proj",
    )(x2d, g, w_all, qg, kg)


def _forget_cum_body(sm_ref, fb_ref, cs_ref, ct_ref, carry_ref):
    @pl.when(pl.program_id(1) == 0)
    def _():
        carry_ref[...] = jnp.zeros_like(carry_ref)

    n = sm_ref.shape[1]
    log_f = -_softplus(-(sm_ref[0] + fb_ref[...]))
    c = _tri_cumsum(log_f, n) + carry_ref[...]
    cs_ref[0] = c
    ct_ref[0] = c.T[0:ct_ref.shape[1], :]
    carry_ref[...] = c[n - 1:n, :]


def _forget_cum(small, fb):
    b, s, w = small.shape
    tc = CUM_ROWS
    return pl.pallas_call(
        _forget_cum_body,
        grid=(b, s // tc),
        in_specs=[pl.BlockSpec((1, tc, w), lambda i, j: (i, j, 0)),
                  pl.BlockSpec((1, w), lambda i, j: (0, 0))],
        out_specs=[pl.BlockSpec((1, tc, w), lambda i, j: (i, j, 0)),
                   pl.BlockSpec((1, V7X_SUBLANES, tc), lambda i, j: (i, 0, j))],
        out_shape=[jax.ShapeDtypeStruct((b, s, w), F32),
                   jax.ShapeDtypeStruct((b, V7X_SUBLANES, s), F32)],
        scratch_shapes=[pltpu.VMEM((1, w), F32)],
        compiler_params=_params(("arbitrary", "arbitrary")),
        name="forget_cum",
    )(small, fb)


def _attn_body(q_ref, k_ref, v_ref, cs_ref, ct_ref, o_ref):
    head = pl.program_id(1)
    s_len = q_ref.shape[1]
    t = ATT_BLOCK
    lane = lax.broadcasted_iota(jnp.int32, (t, cs_ref.shape[-1]), 1)
    row = lax.broadcasted_iota(jnp.int32, (t, t), 0)
    col = lax.broadcasted_iota(jnp.int32, (t, t), 1)
    causal = col <= row
    for qi in range(s_len // t):
        rows = slice(qi * t, (qi + 1) * t)
        q = q_ref[0, rows, :]
        cq = jnp.sum(jnp.where(lane == head, cs_ref[0, rows, :], 0.0), axis=-1, keepdims=True)
        m = l = acc = None
        for ki in range(qi + 1):
            cols = slice(ki * t, (ki + 1) * t)
            s = lax.dot_general(q, k_ref[0, cols, :], (((1,), (1,)), ((), ())), preferred_element_type=F32)
            s = (s + cq) - ct_ref[0, pl.ds(head, 1), cols]
            if ki == qi:
                s = jnp.where(causal, s, -jnp.inf)
            m_blk = jnp.max(s, axis=-1, keepdims=True)
            if ki == 0:
                m = m_blk
                pmat = jnp.exp(s - m)
                l = jnp.sum(pmat, axis=-1, keepdims=True)
                acc = _dot(pmat.astype(BF16), v_ref[0, cols, :])
            else:
                m_new = jnp.maximum(m, m_blk)
                alpha = jnp.exp(m - m_new)
                pmat = jnp.exp(s - m_new)
                l = alpha * l + jnp.sum(pmat, axis=-1, keepdims=True)
                acc = alpha * acc + _dot(pmat.astype(BF16), v_ref[0, cols, :])
                m = m_new
        o_ref[0, rows, :] = (acc / l).astype(o_ref.dtype)


def _attention(q, k, v, cum_s, cum_t):
    b, s, w = q.shape
    hd = ATT_HEAD_DIM
    heads = w // hd
    head_blk = pl.BlockSpec((1, s, hd), lambda bi, h: (bi, 0, h))
    return pl.pallas_call(
        _attn_body,
        grid=(b, heads),
        in_specs=[head_blk, head_blk, head_blk,
                  pl.BlockSpec((1, s, cum_s.shape[-1]), lambda bi, h: (bi, 0, 0)),
                  pl.BlockSpec((1, cum_t.shape[1], s), lambda bi, h: (bi, 0, 0))],
        out_specs=head_blk,
        out_shape=jax.ShapeDtypeStruct((b, s, w), BF16),
        compiler_params=_params(("arbitrary", "arbitrary")),
        name="attention",
    )(q, k, v, cum_s, cum_t)


def _ssd_body(xbc_ref, z_ref, sm_ref, cw_ref, cb_ref, dtb_ref, alog_ref, dskip_ref, ng_ref,
              o_ref, ext_ref, state_ref, y_ref, *, dt_lane0):
    t = xbc_ref.shape[1]
    width = z_ref.shape[-1]
    lanes = V7X_LANES
    n_pairs = width // lanes
    pairs_per_group = n_pairs // SSD_GROUPS
    tail = V7X_SUBLANES

    @pl.when(pl.program_id(1) == 0)
    def _():
        ext_ref[0:tail, :] = jnp.zeros((tail, ext_ref.shape[1]), F32)
        state_ref[...] = jnp.zeros_like(state_ref)

    ext_ref[tail:tail + t, :] = xbc_ref[0]
    conv = cb_ref[...]
    for tap in range(CONV_WIDTH):
        off = tail - (CONV_WIDTH - 1) + tap
        conv = conv + ext_ref[off:off + t, :] * cw_ref[tap:tap + 1, :]
    ext_ref[0:tail, :] = ext_ref[t:t + tail, :]
    conv = _silu(conv)

    sm = sm_ref[0]
    dt = _softplus(sm + dtb_ref[...])
    d_a = dt * (-jnp.exp(alog_ref[...]))
    a_cs = _tri_cumsum(d_a, t)
    a_cs_t = a_cs.T
    a_last = a_cs[t - 1:t, :]

    row = lax.broadcasted_iota(jnp.int32, (t, t), 0)
    col = lax.broadcasted_iota(jnp.int32, (t, t), 1)
    causal = row >= col
    lane = lax.broadcasted_iota(jnp.int32, (t, lanes), 1)
    first_half = lane < SSD_HEAD_DIM
    lane1 = lax.broadcasted_iota(jnp.int32, (1, lanes), 1)

    b_off = width
    c_off = width + SSD_GROUPS * SSD_STATE
    sumsq = jnp.zeros((t, 1), F32)
    for g in range(SSD_GROUPS):
        b_g = conv[:, b_off + g * SSD_STATE:b_off + (g + 1) * SSD_STATE].astype(BF16)
        c_g = conv[:, c_off + g * SSD_STATE:c_off + (g + 1) * SSD_STATE].astype(BF16)
        cb = lax.dot_general(c_g, b_g, (((1,), (1,)), ((), ())), preferred_element_type=F32)
        for jp in range(pairs_per_group):
            pr = g * pairs_per_group + jp
            h0 = dt_lane0 + 2 * pr
            xs = conv[:, pr * lanes:(pr + 1) * lanes]
            a_pair = jnp.where(first_half, a_cs[:, h0:h0 + 1], a_cs[:, h0 + 1:h0 + 2])
            dt_pair = jnp.where(first_half, dt[:, h0:h0 + 1], dt[:, h0 + 1:h0 + 2])
            a_last_pair = jnp.where(lane1 < SSD_HEAD_DIM, a_last[:, h0:h0 + 1], a_last[:, h0 + 1:h0 + 2])
            x_dt = xs * dt_pair
            y = jnp.zeros((t, lanes), F32)
            for half in range(2):
                hh = h0 + half
                diff = a_cs[:, hh:hh + 1] - a_cs_t[hh:hh + 1, :]
                wmat = cb * jnp.exp(jnp.where(causal, diff, -jnp.inf))
                keep = first_half if half == 0 else jnp.logical_not(first_half)
                y = y + _dot(wmat.astype(BF16), jnp.where(keep, x_dt, 0.0).astype(BF16))
            st = state_ref[pr]
            y = y + _dot(c_g, st.astype(BF16)) * jnp.exp(a_pair)
            x_dec = (x_dt * jnp.exp(a_last_pair - a_pair)).astype(BF16)
            upd = lax.dot_general(b_g, x_dec, (((0,), (0,)), ((), ())), preferred_element_type=F32)
            state_ref[pr] = st * jnp.exp(a_last_pair) + upd
            y = y + xs * dskip_ref[:, pr * lanes:(pr + 1) * lanes]
            zz = z_ref[0, :, pr * lanes:(pr + 1) * lanes]
            y = y * _silu(zz)
            sumsq = sumsq + jnp.sum(y * y, axis=-1, keepdims=True)
            y_ref[:, pr * lanes:(pr + 1) * lanes] = y
    scale = lax.rsqrt(sumsq / width + EPS)
    o_ref[0] = (y_ref[...] * scale * ng_ref[...]).astype(o_ref.dtype)


def _ssd(xbc, z, small, conv_w, conv_b, dtb, alog, dskip, norm_g, *, dt_lane0):
    b, s, cc = xbc.shape
    width = z.shape[-1]
    t = SSD_ROWS
    n_pairs = width // V7X_LANES
    blk = lambda i, j: (i, j, 0)
    fixed = lambda i, j: (0, 0)
    return pl.pallas_call(
        functools.partial(_ssd_body, dt_lane0=dt_lane0),
        grid=(b, s // t),
        in_specs=[
            pl.BlockSpec((1, t, cc), blk),
            pl.BlockSpec((1, t, width), blk),
            pl.BlockSpec((1, t, small.shape[-1]), blk),
            pl.BlockSpec(conv_w.shape, fixed),
            pl.BlockSpec(conv_b.shape, fixed),
            pl.BlockSpec(dtb.shape, fixed),
            pl.BlockSpec(alog.shape, fixed),
            pl.BlockSpec(dskip.shape, fixed),
            pl.BlockSpec(norm_g.shape, fixed),
        ],
        out_specs=pl.BlockSpec((1, t, width), blk),
        out_shape=jax.ShapeDtypeStruct((b, s, width), BF16),
        scratch_shapes=[
            pltpu.VMEM((t + V7X_SUBLANES, cc), F32),
            pltpu.VMEM((n_pairs, SSD_STATE, V7X_LANES), F32),
            pltpu.VMEM((t, width), F32),
        ],
        compiler_params=_params(("arbitrary", "arbitrary")),
        name="ssd",
    )(xbc, z, small, conv_w, conv_b, dtb, alog, dskip, norm_g)


def _out_proj_body(att_ref, ssd_ref, x_ref, wo_ref, g_ref, wrh_ref, wrl_ref, br_ref,
                   x1_ref, idx_ref, gate_ref):
    aw = att_ref.shape[-1]
    d = x_ref.shape[-1]
    tm = x_ref.shape[0]
    for c0 in range(0, d, PROJ_COLS):
        y = _dot(att_ref[...], wo_ref[0:aw, c0:c0 + PROJ_COLS])
        y = y + _dot(ssd_ref[...], wo_ref[aw:, c0:c0 + PROJ_COLS])
        x1_ref[:, c0:c0 + PROJ_COLS] = x_ref[:, c0:c0 + PROJ_COLS] + y
    h2 = _rms(x1_ref[...], g_ref[...], EPS)
    h_hi = h2.astype(BF16)
    h_lo = (h2 - h_hi.astype(F32)).astype(BF16)
    logits = _dot(h_hi, wrh_ref[...]) + _dot(h_lo, wrh_ref[...]) + _dot(h_hi, wrl_ref[...]) + br_ref[...]
    lane = lax.broadcasted_iota(jnp.int32, (tm, V7X_LANES), 1)
    logits = jnp.where(lane < N_EXPERTS, logits, -jnp.inf)
    lane_f = lane.astype(F32)
    idx_out = jnp.zeros((tm, V7X_LANES), F32)
    vals = []
    for kk in range(TOP_K):
        m = jnp.max(logits, axis=-1, keepdims=True)
        sel = jnp.min(jnp.where(logits == m, lane_f, float(V7X_LANES)), axis=-1, keepdims=True)
        idx_out = jnp.where(lane == kk, sel, idx_out)
        logits = jnp.where(lane_f == sel, -jnp.inf, logits)
        vals.append(m)
    exps = [jnp.exp(v - vals[0]) for v in vals]
    denom = exps[0]
    for e in exps[1:]:
        denom = denom + e
    gate_out = jnp.zeros((tm, V7X_LANES), F32)
    for kk in range(TOP_K):
        gate_out = jnp.where(lane == kk, exps[kk] / denom, gate_out)
    idx_ref[...] = idx_out.astype(jnp.int32)
    gate_ref[...] = gate_out


def _out_proj(att, ssd, x2d, wo, g, wr_hi, wr_lo, br):
    n, d = x2d.shape
    tm = PROJ_ROWS
    row = lambda i: (i, 0)
    fixed = lambda i: (0, 0)
    return pl.pallas_call(
        _out_proj_body,
        grid=(n // tm,),
        in_specs=[
            pl.BlockSpec((tm, att.shape[-1]), row),
            pl.BlockSpec((tm, ssd.shape[-1]), row),
            pl.BlockSpec((tm, d), row),
            pl.BlockSpec(wo.shape, fixed, pipeline_mode=pl.Buffered(1)),
            pl.BlockSpec((1, d), fixed),
            pl.BlockSpec(wr_hi.shape, fixed),
            pl.BlockSpec(wr_lo.shape, fixed),
            pl.BlockSpec((1, V7X_LANES), fixed),
        ],
        out_specs=[pl.BlockSpec((tm, d), row),
                   pl.BlockSpec((tm, V7X_LANES), row),
                   pl.BlockSpec((tm, V7X_LANES), row)],
        out_shape=[jax.ShapeDtypeStruct((n, d), F32),
                   jax.ShapeDtypeStruct((n, V7X_LANES), jnp.int32),
                   jax.ShapeDtypeStruct((n, V7X_LANES), F32)],
        compiler_params=_params(("arbitrary",)),
        name="out_proj",
    )(att, ssd, x2d, wo, g, wr_hi, wr_lo, br)


def _dispatch_body(plo_ref, phi_ref, dest_ref, x1_ref, g_ref, xs_hbm, stage_ref, zero_ref, sems, zero_sem):
    i = pl.program_id(0)
    n = pl.num_programs(0)
    slot = i % 2
    tm = x1_ref.shape[0]
    sub = V7X_SUBLANES

    def drain(s):
        for _ in range(TOP_K):
            pltpu.make_async_copy(stage_ref.at[s], stage_ref.at[s], sems.at[s]).wait()

    @pl.when(i >= 2)
    def _():
        drain(slot)

    stage_ref[slot] = _rms(x1_ref[...], g_ref[...], EPS)

    def body(r0, carry):
        base = pl.multiple_of(r0 * sub, sub)
        for u in range(sub):
            for kk in range(TOP_K):
                dst = dest_ref[0, 0, (base + u) * TOP_K + kk]
                pltpu.make_async_copy(stage_ref.at[slot, pl.ds(base + u, 1), :],
                                      xs_hbm.at[pl.ds(dst, 1), :], sems.at[slot]).start()
        return carry
    lax.fori_loop(0, tm // sub, body, 0)

    def pad_fill(wait):
        def go(cp):
            if wait:
                cp.wait()
            else:
                cp.start()

        def per_expert(e, carry):
            lo = plo_ref[e]
            hi = phi_ref[e]
            aligned = jnp.minimum(hi, lax.shift_left(lax.shift_right_logical(lo + (sub - 1), 3), 3))

            def one(s, c2):
                go(pltpu.make_async_copy(zero_ref.at[pl.ds(0, 1), :], xs_hbm.at[pl.ds(s, 1), :], zero_sem))
                return c2
            lax.fori_loop(lo, aligned, one, 0)

            def eight(q, c2):
                s8 = pl.multiple_of(aligned + q * sub, sub)
                go(pltpu.make_async_copy(zero_ref, xs_hbm.at[pl.ds(s8, sub), :], zero_sem))
                return c2
            lax.fori_loop(0, lax.shift_right_logical(hi - aligned, 3), eight, 0)
            return carry
        lax.fori_loop(0, plo_ref.shape[0], per_expert, 0)

    @pl.when(i == n - 1)
    def _():
        zero_ref[...] = jnp.zeros_like(zero_ref)
        pad_fill(wait=False)
        drain(slot)
        drain(1 - slot)
        pad_fill(wait=True)


def _dispatch(pad_lo, pad_hi, dest, x1, g, n_slots):
    n, d = x1.shape
    tm = DISP_ROWS
    n_tiles = n // tm
    assert n_tiles >= 2 and V7X_SUBLANES == 8
    dest3 = dest.reshape(n_tiles, 1, tm * TOP_K)
    grid_spec = pltpu.PrefetchScalarGridSpec(
        num_scalar_prefetch=2,
        grid=(n_tiles,),
        in_specs=[
            pl.BlockSpec((1, 1, tm * TOP_K), lambda i, lo, hi: (i, 0, 0), memory_space=pltpu.SMEM),
            pl.BlockSpec((tm, d), lambda i, lo, hi: (i, 0)),
            pl.BlockSpec((1, d), lambda i, lo, hi: (0, 0)),
        ],
        out_specs=pl.BlockSpec(memory_space=pl.ANY),
        scratch_shapes=[pltpu.VMEM((2, tm, d), F32),
                        pltpu.VMEM((V7X_SUBLANES, d), F32),
                        pltpu.SemaphoreType.DMA((2,)),
                        pltpu.SemaphoreType.DMA(())],
    )
    return pl.pallas_call(
        _dispatch_body,
        grid_spec=grid_spec,
        out_shape=jax.ShapeDtypeStruct((n_slots, d), F32),
        compiler_params=_params(("arbitrary",)),
        name="dispatch",
    )(pad_lo, pad_hi, dest3, x1, g)


def _experts_body(ge_ref, gb_ref, gn_ref, nu_ref, xs_hbm, wg_ref, bg_ref, wu_ref, bu_ref, wd_ref,
                  bd_ref, y_hbm, xraw_ref, xs_ref, acc_ref, zero_ref, in_sems, out_sem, zero_sem):
    grp = pl.program_id(0)
    c = pl.program_id(1)
    n_chunks = pl.num_programs(1)
    nblk = gn_ref[grp]
    blk0 = gb_ref[grp]
    rows = MOE_BLOCK
    n_blocks = y_hbm.shape[0] // rows
    zrows = zero_ref.shape[0]

    def block_rows(j):
        return pl.ds(pl.multiple_of(j * rows, rows), rows)

    def zero_fill(wait):
        def body(blk, carry):
            for q in range(rows // zrows):
                dst = pl.ds(pl.multiple_of(blk * rows + q * zrows, zrows), zrows)
                cp = pltpu.make_async_copy(zero_ref, y_hbm.at[dst, :], zero_sem)
                if wait:
                    cp.wait()
                else:
                    cp.start()
            return carry
        lax.fori_loop(nu_ref[0], n_blocks, body, 0)

    @pl.when(jnp.logical_and(grp == 0, c == 0))
    def _():
        zero_ref[...] = jnp.zeros_like(zero_ref)
        zero_fill(wait=False)

    def in_copy(j):
        src = pl.ds(pl.multiple_of((blk0 + j) * rows, rows), rows)
        return pltpu.make_async_copy(xs_hbm.at[src, :], xraw_ref.at[j % 2], in_sems.at[j % 2])

    @pl.when(jnp.logical_and(nblk > 0, c == 0))
    def _():
        in_copy(0).start()

        def body(j, carry):
            @pl.when(j + 1 < nblk)
            def _():
                in_copy(j + 1).start()
            in_copy(j).wait()
            xs_ref[block_rows(j), :] = xraw_ref[j % 2].astype(BF16)
            acc_ref[block_rows(j), :] = jnp.broadcast_to(bd_ref[0], (rows, acc_ref.shape[1]))
            return carry
        lax.fori_loop(0, nblk, body, 0)

    @pl.when(nblk > 0)
    def _():
        wg = wg_ref[0].astype(BF16)
        wu = wu_ref[0].astype(BF16)
        wd = wd_ref[0].astype(BF16)

        def body(j, carry):
            xs = xs_ref[block_rows(j), :]
            gg = _dot(xs, wg) + bg_ref[0]
            uu = _dot(xs, wu) + bu_ref[0]
            gg = jnp.minimum(gg, SWIGLU_LIMIT)
            uu = jnp.clip(uu, -SWIGLU_LIMIT, SWIGLU_LIMIT)
            glu = gg * (1.0 / (1.0 + jnp.exp(-SWIGLU_ALPHA * gg)))
            act = ((uu + 1.0) * glu).astype(BF16)
            acc_ref[block_rows(j), :] += _dot(act, wd)
            return carry
        lax.fori_loop(0, nblk, body, 0)

    @pl.when(jnp.logical_and(nblk > 0, c == n_chunks - 1))
    def _():
        def out_copy(j):
            dst = pl.ds(pl.multiple_of((blk0 + j) * rows, rows), rows)
            return pltpu.make_async_copy(acc_ref.at[block_rows(j), :], y_hbm.at[dst, :], out_sem)

        def start(j, carry):
            out_copy(j).start()
            return carry

        def wait(j, carry):
            out_copy(j).wait()
            return carry
        lax.fori_loop(0, nblk, start, 0)
        lax.fori_loop(0, nblk, wait, 0)

    @pl.when(jnp.logical_and(grp == pl.num_programs(0) - 1, c == n_chunks - 1))
    def _():
        zero_fill(wait=True)


def _experts(group_e, group_blk, group_n, n_used, xs, wg, bg, wu, bu, wd, bd):
    n_slots, d = xs.shape
    dff = wg.shape[-1]
    n_chunks = dff // FF_CHUNK
    n_groups = group_e.shape[0]
    bg3 = bg.reshape(N_EXPERTS, 1, dff)
    bu3 = bu.reshape(N_EXPERTS, 1, dff)
    bd3 = bd.reshape(N_EXPERTS, 1, d)

    def chunk(i, c, gn):
        return jnp.where(gn[i] > 0, c, n_chunks - 1)

    grid_spec = pltpu.PrefetchScalarGridSpec(
        num_scalar_prefetch=4,
        grid=(n_groups, n_chunks),
        in_specs=[
            pl.BlockSpec(memory_space=pl.ANY),
            pl.BlockSpec((1, d, FF_CHUNK), lambda i, c, ge, gb, gn, nu: (ge[i], 0, chunk(i, c, gn))),
            pl.BlockSpec((1, 1, FF_CHUNK), lambda i, c, ge, gb, gn, nu: (ge[i], 0, chunk(i, c, gn))),
            pl.BlockSpec((1, d, FF_CHUNK), lambda i, c, ge, gb, gn, nu: (ge[i], 0, chunk(i, c, gn))),
            pl.BlockSpec((1, 1, FF_CHUNK), lambda i, c, ge, gb, gn, nu: (ge[i], 0, chunk(i, c, gn))),
            pl.BlockSpec((1, FF_CHUNK, d), lambda i, c, ge, gb, gn, nu: (ge[i], chunk(i, c, gn), 0)),
            pl.BlockSpec((1, 1, d), lambda i, c, ge, gb, gn, nu: (ge[i], 0, 0)),
        ],
        out_specs=pl.BlockSpec(memory_space=pl.ANY),
        scratch_shapes=[pltpu.VMEM((2, MOE_BLOCK, d), F32),
                        pltpu.VMEM((GROUP_BLOCKS * MOE_BLOCK, d), BF16),
                        pltpu.VMEM((GROUP_BLOCKS * MOE_BLOCK, d), F32),
                        pltpu.VMEM((ZERO_ROWS, d), F32),
                        pltpu.SemaphoreType.DMA((2,)),
                        pltpu.SemaphoreType.DMA(()),
                        pltpu.SemaphoreType.DMA(())],
    )
    return pl.pallas_call(
        _experts_body,
        grid_spec=grid_spec,
        out_shape=jax.ShapeDtypeStruct((n_slots, d), F32),
        compiler_params=_params(("arbitrary", "arbitrary")),
        name="experts",
    )(group_e, group_blk, group_n, n_used, xs, wg, bg3, wu, bu3, wd, bd3)


def _combine_body(dest_ref, dest_next_ref, x1_ref, gate_ref, y_hbm, o_ref, buf_ref, sems):
    tm = x1_ref.shape[0]
    i = pl.program_id(0)
    slot = i % 2

    def start_gather(ids_ref, s):
        sub = V7X_SUBLANES

        def body(r0, carry):
            base = pl.multiple_of(r0 * sub, sub)
            for u in range(sub):
                for kk in range(TOP_K):
                    src = ids_ref[0, 0, (base + u) * TOP_K + kk]
                    pltpu.make_async_copy(y_hbm.at[pl.ds(src, 1), :],
                                          buf_ref.at[s, kk, pl.ds(base + u, 1), :], sems.at[s]).start()
            return carry
        lax.fori_loop(0, tm // sub, body, 0)

    @pl.when(i == 0)
    def _():
        start_gather(dest_ref, 0)

    @pl.when(i + 1 < pl.num_programs(0))
    def _():
        start_gather(dest_next_ref, 1 - slot)

    pltpu.make_async_copy(buf_ref.at[slot], buf_ref.at[slot], sems.at[slot]).wait()
    acc = x1_ref[...]
    for kk in range(TOP_K):
        acc = acc + gate_ref[:, kk:kk + 1] * buf_ref[slot, kk]
    o_ref[...] = acc


def _combine(dest, x1, gates, yb):
    n, d = x1.shape
    tm = COMB_ROWS
    n_tiles = n // tm
    dest3 = dest.reshape(n_tiles, 1, tm * TOP_K)
    row = lambda i: (i, 0)
    return pl.pallas_call(
        _combine_body,
        grid=(n_tiles,),
        in_specs=[
            pl.BlockSpec((1, 1, tm * TOP_K), lambda i: (i, 0, 0), memory_space=pltpu.SMEM),
            pl.BlockSpec((1, 1, tm * TOP_K), lambda i: (jnp.minimum(i + 1, n_tiles - 1), 0, 0),
                         memory_space=pltpu.SMEM),
            pl.BlockSpec((tm, d), row),
            pl.BlockSpec((tm, V7X_LANES), row),
            pl.BlockSpec(memory_space=pl.ANY),
        ],
        out_specs=pl.BlockSpec((tm, d), row),
        out_shape=jax.ShapeDtypeStruct((n, d), F32),
        scratch_shapes=[pltpu.VMEM((2, TOP_K, tm, d), F32), pltpu.SemaphoreType.DMA((2,))],
        compiler_params=_params(("arbitrary",)),
        name="combine",
    )(dest3, dest3, x1, gates, yb)


def _slot_layout(top_idx, n_tok):
    nk = n_tok * TOP_K
    onehot = (top_idx[:, :, None] == jnp.arange(N_EXPERTS, dtype=jnp.int32)[None, None, :])
    member = jnp.sum(onehot.astype(jnp.int32), axis=1)
    csum = jnp.cumsum(member, axis=0)
    counts = csum[-1]
    rank = jnp.take_along_axis(csum - member, top_idx, axis=1)
    padded = ((counts + MOE_BLOCK - 1) // MOE_BLOCK) * MOE_BLOCK
    pend = jnp.cumsum(padded)
    pstart = pend - padded
    dest = (pstart[top_idx] + rank).astype(jnp.int32)
    n_blocks = -(-nk // MOE_BLOCK) + N_EXPERTS
    n_slots = n_blocks * MOE_BLOCK
    pad_lo = jnp.concatenate([pstart + counts, pend[-1:]]).astype(jnp.int32)
    pad_hi = jnp.concatenate([pend, jnp.full((1,), n_slots, pend.dtype)]).astype(jnp.int32)
    eblk = (padded // MOE_BLOCK).astype(jnp.int32)
    gcount = (eblk + GROUP_BLOCKS - 1) // GROUP_BLOCKS
    gend = jnp.cumsum(gcount)
    gstart = gend - gcount
    n_groups = (n_blocks + N_EXPERTS * (GROUP_BLOCKS - 1)) // GROUP_BLOCKS
    gid = jnp.arange(n_groups, dtype=jnp.int32)
    valid = gid < gend[-1]
    last_gid = jnp.maximum(gend[-1] - 1, 0)
    gsel = jnp.where(valid, gid, last_gid)
    group_e = jnp.minimum(jnp.sum((gsel[:, None] >= gend[None, :]).astype(jnp.int32), axis=1), N_EXPERTS - 1)
    local = gsel - gstart[group_e]
    group_blk = (pstart[group_e] // MOE_BLOCK + local * GROUP_BLOCKS).astype(jnp.int32)
    group_n = jnp.where(valid, jnp.clip(eblk[group_e] - local * GROUP_BLOCKS, 0, GROUP_BLOCKS), 0).astype(jnp.int32)
    n_used = (pend[-1] // MOE_BLOCK).astype(jnp.int32).reshape(1)
    return dest, n_slots, pad_lo, pad_hi, group_e.astype(jnp.int32), group_blk, group_n, n_used


def _pad_lanes(v, lane0=0):
    out = jnp.zeros((1, V7X_LANES), F32)
    return out.at[0, lane0:lane0 + v.shape[0]].set(v.astype(F32))


def kernel(x, attn_norm_g, w_in, forget_bias, q_norm_g, k_norm_g, conv_w, conv_b, dt_bias, a_log, d_skip,
           ssd_norm_g, w_out, ffn_norm_g, w_router, b_router, w_gate, b_gate, w_up, b_up, w_down, b_down):
    b, s, d = x.shape
    depth = attn_norm_g.shape[0]
    att_w = ATT_HEADS * ATT_HEAD_DIM
    ssd_w = w_out.shape[1] - att_w
    ssd_heads = ssd_w // SSD_HEAD_DIM
    conv_ch = ssd_w + 2 * SSD_GROUPS * SSD_STATE
    n_tok = b * s
    dt_lane0 = ATT_HEADS
    o_f = 3 * att_w
    o_z = o_f + ATT_HEADS
    o_xbc = o_z + ssd_w
    o_dt = o_xbc + conv_ch

    x2d = x.reshape(n_tok, d)
    for layer in range(depth):
        wl = w_in[layer]
        w_all = jnp.concatenate(
            [wl[:, :o_f], wl[:, o_z:o_dt], wl[:, o_f:o_z], wl[:, o_dt:o_dt + ssd_heads],
             jnp.zeros((d, SMALL_COLS - ATT_HEADS - ssd_heads), F32)], axis=1).astype(BF16)

        q, k, v, z, xbc, small = _in_proj(
            x2d, attn_norm_g[layer][None, :], w_all, q_norm_g[layer][None, :], k_norm_g[layer][None, :],
            att_w=att_w, ssd_w=ssd_w, conv_ch=conv_ch)

        small3 = small.reshape(b, s, SMALL_COLS)
        cum_s, cum_t = _forget_cum(small3, _pad_lanes(forget_bias[layer]))
        att = _attention(q.reshape(b, s, att_w), k.reshape(b, s, att_w), v.reshape(b, s, att_w), cum_s, cum_t)

        ssd = _ssd(xbc.reshape(b, s, conv_ch), z.reshape(b, s, ssd_w), small3,
                   conv_w[layer], conv_b[layer][None, :],
                   _pad_lanes(dt_bias[layer], dt_lane0), _pad_lanes(a_log[layer], dt_lane0),
                   jnp.repeat(d_skip[layer], SSD_HEAD_DIM)[None, :], ssd_norm_g[layer][None, :],
                   dt_lane0=dt_lane0)

        wr = jnp.zeros((d, V7X_LANES), F32).at[:, :N_EXPERTS].set(w_router[layer])
        wr_hi = wr.astype(BF16)
        wr_lo = (wr - wr_hi.astype(F32)).astype(BF16)
        x1, idx_pad, gate_pad = _out_proj(
            att.reshape(n_tok, att_w), ssd.reshape(n_tok, ssd_w), x2d, w_out[layer].astype(BF16),
            ffn_norm_g[layer][None, :], wr_hi, wr_lo, _pad_lanes(b_router[layer]))

        dest, n_slots, pad_lo, pad_hi, group_e, group_blk, group_n, n_used = _slot_layout(idx_pad[:, :TOP_K], n_tok)
        dest = dest.reshape(-1)
        xs = _dispatch(pad_lo, pad_hi, dest, x1, ffn_norm_g[layer][None, :], n_slots)
        yb = _experts(group_e, group_blk, group_n, n_used, xs,
                      w_gate[layer], b_gate[layer], w_up[layer], b_up[layer], w_down[layer], b_down[layer])
        x2d = _combine(dest, x1, gate_pad, yb)
    return x2d.reshape(b, s, d)
```

```python
import functools
import math

import jax
import jax.numpy as jnp
from jax import lax
from jax.experimental import pallas as pl
from jax.experimental.pallas import tpu as pltpu

F32 = jnp.float32
BF16 = jnp.bfloat16

ATT_HEADS = 8
ATT_HEAD_DIM = 128
SSD_HEAD_DIM = 64
SSD_GROUPS = 2
SSD_STATE = 128
CONV_WIDTH = 4
N_EXPERTS = 32
TOP_K = 4
SWIGLU_ALPHA = 1.702
SWIGLU_LIMIT = 7.0
EPS = 1e-6

V7X_LANES = 128
V7X_SUBLANES = 8
V7X_VMEM_BYTES = 64 * 1024 * 1024
VMEM_LIMIT_BYTES = 58 * 1024 * 1024

PROJ_ROWS = 256
PROJ_COLS = 512
CUM_ROWS = 256
ATT_BLOCK = 512
SSD_ROWS = 256
DISP_ROWS = 256
MOE_BLOCK = 256
GROUP_BLOCKS = 5
ZERO_ROWS = 32
FF_CHUNK = 512
COMB_ROWS = 128
SMALL_COLS = V7X_LANES


def _params(dims):
    return pltpu.CompilerParams(dimension_semantics=dims, vmem_limit_bytes=VMEM_LIMIT_BYTES)


def _rms(x, g, eps):
    return x * lax.rsqrt(jnp.mean(x * x, axis=-1, keepdims=True) + eps) * g


def _split3(x):
    x1 = x.astype(BF16)
    r1 = x - x1.astype(F32)
    x2 = r1.astype(BF16)
    r2 = r1 - x2.astype(F32)
    return x1, x2, r2.astype(BF16)


def _dot(a, b):
    return jnp.dot(a, b, preferred_element_type=F32)


def _tri_cumsum(x, n):
    row = lax.broadcasted_iota(jnp.int32, (n, n), 0)
    col = lax.broadcasted_iota(jnp.int32, (n, n), 1)
    tri = (row >= col).astype(BF16)
    x1, x2, x3 = _split3(x)
    return _dot(tri, x1) + _dot(tri, x2) + _dot(tri, x3)


def _softplus(x):
    return jnp.maximum(x, 0.0) + jnp.log1p(jnp.exp(-jnp.abs(x)))


def _silu(x):
    return x * (1.0 / (1.0 + jnp.exp(-x)))


def _in_proj_body(x_ref, g_ref, w_ref, qg_ref, kg_ref,
                  q_ref, k_ref, v_ref, z_ref, xbc_ref, sm_ref, *, head_dim, q_scale):
    att_w = q_ref.shape[-1]
    ssd_w = z_ref.shape[-1]
    conv_ch = xbc_ref.shape[-1]
    h = _rms(x_ref[...], g_ref[...], EPS).astype(BF16)

    def proj(c0, width):
        return _dot(h, w_ref[:, c0:c0 + width])

    def chunks(width):
        return [(c, min(PROJ_COLS, width - c)) for c in range(0, width, PROJ_COLS)]

    col = 0
    for out_ref, gain_ref, scale in ((q_ref, qg_ref, q_scale), (k_ref, kg_ref, None)):
        for c0, cw in chunks(att_w):
            y = proj(col + c0, cw)
            for j in range(cw // head_dim):
                yn = _rms(y[:, j * head_dim:(j + 1) * head_dim], gain_ref[...], EPS)
                if scale is not None:
                    yn = yn * scale
                out_ref[:, c0 + j * head_dim:c0 + (j + 1) * head_dim] = yn.astype(out_ref.dtype)
        col += att_w
    for out_ref, width in ((v_ref, att_w), (z_ref, ssd_w), (xbc_ref, conv_ch), (sm_ref, SMALL_COLS)):
        for c0, cw in chunks(width):
            out_ref[:, c0:c0 + cw] = proj(col + c0, cw).astype(out_ref.dtype)
        col += width


def _in_proj(x2d, g, w_all, qg, kg, *, att_w, ssd_w, conv_ch):
    n, d = x2d.shape
    tm = PROJ_ROWS
    row = lambda i: (i, 0)
    fixed = lambda i: (0, 0)
    outs = [(att_w, BF16), (att_w, BF16), (att_w, BF16), (ssd_w, F32), (conv_ch, F32), (SMALL_COLS, F32)]
    return pl.pallas_call(
        functools.partial(_in_proj_body, head_dim=ATT_HEAD_DIM, q_scale=1.0 / math.sqrt(ATT_HEAD_DIM)),
        grid=(n // tm,),
        in_specs=[
            pl.BlockSpec((tm, d), row),
            pl.BlockSpec((1, d), fixed),
            pl.BlockSpec(w_all.shape, fixed, pipeline_mode=pl.Buffered(1)),
            pl.BlockSpec((1, ATT_HEAD_DIM), fixed),
            pl.BlockSpec((1, ATT_HEAD_DIM), fixed),
        ],
        out_specs=[pl.BlockSpec((tm, w), row) for w, _ in outs],
        out_shape=[jax.ShapeDtypeStruct((n, w), dt) for w, dt in outs],
        compiler_params=_params(("arbitrary",)),
        name="in_proj",
    )(x2d, g, w_all, qg, kg)


def _forget_cum_body(sm_ref, fb_ref, cs_ref, ct_ref, carry_ref):
    @pl.when(pl.program_id(1) == 0)
    def _():
        carry_ref[...] = jnp.zeros_like(carry_ref)

    n = sm_ref.shape[1]
    log_f = -_softplus(-(sm_ref[0] + fb_ref[...]))
    c = _tri_cumsum(log_f, n) + carry_ref[...]
    cs_ref[0] = c
    ct_ref[0] = c.T[0:ct_ref.shape[1], :]
    carry_ref[...] = c[n - 1:n, :]


def _forget_cum(small, fb):
    b, s, w = small.shape
    tc = CUM_ROWS
    return pl.pallas_call(
        _forget_cum_body,
        grid=(b, s // tc),
        in_specs=[pl.BlockSpec((1, tc, w), lambda i, j: (i, j, 0)),
                  pl.BlockSpec((1, w), lambda i, j: (0, 0))],
        out_specs=[pl.BlockSpec((1, tc, w), lambda i, j: (i, j, 0)),
                   pl.BlockSpec((1, V7X_SUBLANES, tc), lambda i, j: (i, 0, j))],
        out_shape=[jax.ShapeDtypeStruct((b, s, w), F32),
                   jax.ShapeDtypeStruct((b, V7X_SUBLANES, s), F32)],
        scratch_shapes=[pltpu.VMEM((1, w), F32)],
        compiler_params=_params(("arbitrary", "arbitrary")),
        name="forget_cum",
    )(small, fb)


def _attn_body(q_ref, k_ref, v_ref, cs_ref, ct_ref, o_ref):
    head = pl.program_id(1)
    s_len = q_ref.shape[1]
    t = ATT_BLOCK
    lane = lax.broadcasted_iota(jnp.int32, (t, cs_ref.shape[-1]), 1)
    row = lax.broadcasted_iota(jnp.int32, (t, t), 0)
    col = lax.broadcasted_iota(jnp.int32, (t, t), 1)
    causal = col <= row
    for qi in range(s_len // t):
        rows = slice(qi * t, (qi + 1) * t)
        q = q_ref[0, rows, :]
        cq = jnp.sum(jnp.where(lane == head, cs_ref[0, rows, :], 0.0), axis=-1, keepdims=True)
        m = l = acc = None
        for ki in range(qi + 1):
            cols = slice(ki * t, (ki + 1) * t)
            s = lax.dot_general(q, k_ref[0, cols, :], (((1,), (1,)), ((), ())), preferred_element_type=F32)
            s = (s + cq) - ct_ref[0, pl.ds(head, 1), cols]
            if ki == qi:
                s = jnp.where(causal, s, -jnp.inf)
            m_blk = jnp.max(s, axis=-1, keepdims=True)
            if ki == 0:
                m = m_blk
                pmat = jnp.exp(s - m)
                l = jnp.sum(pmat, axis=-1, keepdims=True)
                acc = _dot(pmat.astype(BF16), v_ref[0, cols, :])
            else:
                m_new = jnp.maximum(m, m_blk)
                alpha = jnp.exp(m - m_new)
                pmat = jnp.exp(s - m_new)
                l = alpha * l + jnp.sum(pmat, axis=-1, keepdims=True)
                acc = alpha * acc + _dot(pmat.astype(BF16), v_ref[0, cols, :])
                m = m_new
        o_ref[0, rows, :] = (acc / l).astype(o_ref.dtype)


def _attention(q, k, v, cum_s, cum_t):
    b, s, w = q.shape
    hd = ATT_HEAD_DIM
    heads = w // hd
    head_blk = pl.BlockSpec((1, s, hd), lambda bi, h: (bi, 0, h))
    return pl.pallas_call(
        _attn_body,
        grid=(b, heads),
        in_specs=[head_blk, head_blk, head_blk,
                  pl.BlockSpec((1, s, cum_s.shape[-1]), lambda bi, h: (bi, 0, 0)),
                  pl.BlockSpec((1, cum_t.shape[1], s), lambda bi, h: (bi, 0, 0))],
        out_specs=head_blk,
        out_shape=jax.ShapeDtypeStruct((b, s, w), BF16),
        compiler_params=_params(("arbitrary", "arbitrary")),
        name="attention",
    )(q, k, v, cum_s, cum_t)


def _ssd_body(xbc_ref, z_ref, sm_ref, cw_ref, cb_ref, dtb_ref, alog_ref, dskip_ref, ng_ref,
              o_ref, ext_ref, state_ref, y_ref, *, dt_lane0):
    t = xbc_ref.shape[1]
    width = z_ref.shape[-1]
    lanes = V7X_LANES
    n_pairs = width // lanes
    pairs_per_group = n_pairs // SSD_GROUPS
    tail = V7X_SUBLANES

    @pl.when(pl.program_id(1) == 0)
    def _():
        ext_ref[0:tail, :] = jnp.zeros((tail, ext_ref.shape[1]), F32)
        state_ref[...] = jnp.zeros_like(state_ref)

    ext_ref[tail:tail + t, :] = xbc_ref[0]
    conv = cb_ref[...]
    for tap in range(CONV_WIDTH):
        off = tail - (CONV_WIDTH - 1) + tap
        conv = conv + ext_ref[off:off + t, :] * cw_ref[tap:tap + 1, :]
    ext_ref[0:tail, :] = ext_ref[t:t + tail, :]
    conv = _silu(conv)

    sm = sm_ref[0]
    dt = _softplus(sm + dtb_ref[...])
    d_a = dt * (-jnp.exp(alog_ref[...]))
    a_cs = _tri_cumsum(d_a, t)
    a_cs_t = a_cs.T
    a_last = a_cs[t - 1:t, :]

    row = lax.broadcasted_iota(jnp.int32, (t, t), 0)
    col = lax.broadcasted_iota(jnp.int32, (t, t), 1)
    causal = row >= col
    lane = lax.broadcasted_iota(jnp.int32, (t, lanes), 1)
    first_half = lane < SSD_HEAD_DIM
    lane1 = lax.broadcasted_iota(jnp.int32, (1, lanes), 1)

    b_off = width
    c_off = width + SSD_GROUPS * SSD_STATE
    sumsq = jnp.zeros((t, 1), F32)
    for g in range(SSD_GROUPS):
        b_g = conv[:, b_off + g * SSD_STATE:b_off + (g + 1) * SSD_STATE].astype(BF16)
        c_g = conv[:, c_off + g * SSD_STATE:c_off + (g + 1) * SSD_STATE].astype(BF16)
        cb = lax.dot_general(c_g, b_g, (((1,), (1,)), ((), ())), preferred_element_type=F32)
        for jp in range(pairs_per_group):
            pr = g * pairs_per_group + jp
            h0 = dt_lane0 + 2 * pr
            xs = conv[:, pr * lanes:(pr + 1) * lanes]
            a_pair = jnp.where(first_half, a_cs[:, h0:h0 + 1], a_cs[:, h0 + 1:h0 + 2])
            dt_pair = jnp.where(first_half, dt[:, h0:h0 + 1], dt[:, h0 + 1:h0 + 2])
            a_last_pair = jnp.where(lane1 < SSD_HEAD_DIM, a_last[:, h0:h0 + 1], a_last[:, h0 + 1:h0 + 2])
            x_dt = xs * dt_pair
            y = jnp.zeros((t, lanes), F32)
            for half in range(2):
                hh = h0 + half
                diff = a_cs[:, hh:hh + 1] - a_cs_t[hh:hh + 1, :]
                wmat = cb * jnp.exp(jnp.where(causal, diff, -jnp.inf))
                keep = first_half if half == 0 else jnp.logical_not(first_half)
                y = y + _dot(wmat.astype(BF16), jnp.where(keep, x_dt, 0.0).astype(BF16))
            st = state_ref[pr]
            y = y + _dot(c_g, st.astype(BF16)) * jnp.exp(a_pair)
            x_dec = (x_dt * jnp.exp(a_last_pair - a_pair)).astype(BF16)
            upd = lax.dot_general(b_g, x_dec, (((0,), (0,)), ((), ())), preferred_element_type=F32)
            state_ref[pr] = st * jnp.exp(a_last_pair) + upd
            y = y + xs * dskip_ref[:, pr * lanes:(pr + 1) * lanes]
            zz = z_ref[0, :, pr * lanes:(pr + 1) * lanes]
            y = y * _silu(zz)
            sumsq = sumsq + jnp.sum(y * y, axis=-1, keepdims=True)
            y_ref[:, pr * lanes:(pr + 1) * lanes] = y
    scale = lax.rsqrt(sumsq / width + EPS)
    o_ref[0] = (y_ref[...] * scale * ng_ref[...]).astype(o_ref.dtype)


def _ssd(xbc, z, small, conv_w, conv_b, dtb, alog, dskip, norm_g, *, dt_lane0):
    b, s, cc = xbc.shape
    width = z.shape[-1]
    t = SSD_ROWS
    n_pairs = width // V7X_LANES
    blk = lambda i, j: (i, j, 0)
    fixed = lambda i, j: (0, 0)
    return pl.pallas_call(
        functools.partial(_ssd_body, dt_lane0=dt_lane0),
        grid=(b, s // t),
        in_specs=[
            pl.BlockSpec((1, t, cc), blk),
            pl.BlockSpec((1, t, width), blk),
            pl.BlockSpec((1, t, small.shape[-1]), blk),
            pl.BlockSpec(conv_w.shape, fixed),
            pl.BlockSpec(conv_b.shape, fixed),
            pl.BlockSpec(dtb.shape, fixed),
            pl.BlockSpec(alog.shape, fixed),
            pl.BlockSpec(dskip.shape, fixed),
            pl.BlockSpec(norm_g.shape, fixed),
        ],
        out_specs=pl.BlockSpec((1, t, width), blk),
        out_shape=jax.ShapeDtypeStruct((b, s, width), BF16),
        scratch_shapes=[
            pltpu.VMEM((t + V7X_SUBLANES, cc), F32),
            pltpu.VMEM((n_pairs, SSD_STATE, V7X_LANES), F32),
            pltpu.VMEM((t, width), F32),
        ],
        compiler_params=_params(("arbitrary", "arbitrary")),
        name="ssd",
    )(xbc, z, small, conv_w, conv_b, dtb, alog, dskip, norm_g)


def _out_proj_body(att_ref, ssd_ref, x_ref, wo_ref, g_ref, wrh_ref, wrl_ref, br_ref,
                   x1_ref, idx_ref, gate_ref):
    aw = att_ref.shape[-1]
    d = x_ref.shape[-1]
    tm = x_ref.shape[0]
    for c0 in range(0, d, PROJ_COLS):
        y = _dot(att_ref[...], wo_ref[0:aw, c0:c0 + PROJ_COLS])
        y = y + _dot(ssd_ref[...], wo_ref[aw:, c0:c0 + PROJ_COLS])
        x1_ref[:, c0:c0 + PROJ_COLS] = x_ref[:, c0:c0 + PROJ_COLS] + y
    h2 = _rms(x1_ref[...], g_ref[...], EPS)
    h_hi = h2.astype(BF16)
    h_lo = (h2 - h_hi.astype(F32)).astype(BF16)
    logits = _dot(h_hi, wrh_ref[...]) + _dot(h_lo, wrh_ref[...]) + _dot(h_hi, wrl_ref[...]) + br_ref[...]
    lane = lax.broadcasted_iota(jnp.int32, (tm, V7X_LANES), 1)
    logits = jnp.where(lane < N_EXPERTS, logits, -jnp.inf)
    lane_f = lane.astype(F32)
    idx_out = jnp.zeros((tm, V7X_LANES), F32)
    vals = []
    for kk in range(TOP_K):
        m = jnp.max(logits, axis=-1, keepdims=True)
        sel = jnp.min(jnp.where(logits == m, lane_f, float(V7X_LANES)), axis=-1, keepdims=True)
        idx_out = jnp.where(lane == kk, sel, idx_out)
        logits = jnp.where(lane_f == sel, -jnp.inf, logits)
        vals.append(m)
    exps = [jnp.exp(v - vals[0]) for v in vals]
    denom = exps[0]
    for e in exps[1:]:
        denom = denom + e
    gate_out = jnp.zeros((tm, V7X_LANES), F32)
    for kk in range(TOP_K):
        gate_out = jnp.where(lane == kk, exps[kk] / denom, gate_out)
    idx_ref[...] = idx_out.astype(jnp.int32)
    gate_ref[...] = gate_out


def _out_proj(att, ssd, x2d, wo, g, wr_hi, wr_lo, br):
    n, d = x2d.shape
    tm = PROJ_ROWS
    row = lambda i: (i, 0)
    fixed = lambda i: (0, 0)
    return pl.pallas_call(
        _out_proj_body,
        grid=(n // tm,),
        in_specs=[
            pl.BlockSpec((tm, att.shape[-1]), row),
            pl.BlockSpec((tm, ssd.shape[-1]), row),
            pl.BlockSpec((tm, d), row),
            pl.BlockSpec(wo.shape, fixed, pipeline_mode=pl.Buffered(1)),
            pl.BlockSpec((1, d), fixed),
            pl.BlockSpec(wr_hi.shape, fixed),
            pl.BlockSpec(wr_lo.shape, fixed),
            pl.BlockSpec((1, V7X_LANES), fixed),
        ],
        out_specs=[pl.BlockSpec((tm, d), row),
                   pl.BlockSpec((tm, V7X_LANES), row),
                   pl.BlockSpec((tm, V7X_LANES), row)],
        out_shape=[jax.ShapeDtypeStruct((n, d), F32),
                   jax.ShapeDtypeStruct((n, V7X_LANES), jnp.int32),
                   jax.ShapeDtypeStruct((n, V7X_LANES), F32)],
        compiler_params=_params(("arbitrary",)),
        name="out_proj",
    )(att, ssd, x2d, wo, g, wr_hi, wr_lo, br)


def _dispatch_body(plo_ref, phi_ref, dest_ref, x1_ref, g_ref, xs_hbm, stage_ref, zero_ref, sems, zero_sem):
    i = pl.program_id(0)
    n = pl.num_programs(0)
    slot = i % 2
    tm = x1_ref.shape[0]
    sub = V7X_SUBLANES

    def drain(s):
        for _ in range(TOP_K):
            pltpu.make_async_copy(stage_ref.at[s], stage_ref.at[s], sems.at[s]).wait()

    @pl.when(i >= 2)
    def _():
        drain(slot)

    stage_ref[slot] = _rms(x1_ref[...], g_ref[...], EPS)

    def body(r0, carry):
        base = pl.multiple_of(r0 * sub, sub)
        for u in range(sub):
            for kk in range(TOP_K):
                dst = dest_ref[0, 0, (base + u) * TOP_K + kk]
                pltpu.make_async_copy(stage_ref.at[slot, pl.ds(base + u, 1), :],
                                      xs_hbm.at[pl.ds(dst, 1), :], sems.at[slot]).start()
        return carry
    lax.fori_loop(0, tm // sub, body, 0)

    def pad_fill(wait):
        def go(cp):
            if wait:
                cp.wait()
            else:
                cp.start()

        def per_expert(e, carry):
            lo = plo_ref[e]
            hi = phi_ref[e]
            aligned = jnp.minimum(hi, lax.shift_left(lax.shift_right_logical(lo + (sub - 1), 3), 3))

            def one(s, c2):
                go(pltpu.make_async_copy(zero_ref.at[pl.ds(0, 1), :], xs_hbm.at[pl.ds(s, 1), :], zero_sem))
                return c2
            lax.fori_loop(lo, aligned, one, 0)

            def eight(q, c2):
                s8 = pl.multiple_of(aligned + q * sub, sub)
                go(pltpu.make_async_copy(zero_ref, xs_hbm.at[pl.ds(s8, sub), :], zero_sem))
                return c2
            lax.fori_loop(0, lax.shift_right_logical(hi - aligned, 3), eight, 0)
            return carry
        lax.fori_loop(0, plo_ref.shape[0], per_expert, 0)

    @pl.when(i == n - 1)
    def _():
        zero_ref[...] = jnp.zeros_like(zero_ref)
        pad_fill(wait=False)
        drain(slot)
        drain(1 - slot)
        pad_fill(wait=True)


def _dispatch(pad_lo, pad_hi, dest, x1, g, n_slots):
    n, d = x1.shape
    tm = DISP_ROWS
    n_tiles = n // tm
    assert n_tiles >= 2 and V7X_SUBLANES == 8
    dest3 = dest.reshape(n_tiles, 1, tm * TOP_K)
    grid_spec = pltpu.PrefetchScalarGridSpec(
        num_scalar_prefetch=2,
        grid=(n_tiles,),
        in_specs=[
            pl.BlockSpec((1, 1, tm * TOP_K), lambda i, lo, hi: (i, 0, 0), memory_space=pltpu.SMEM),
            pl.BlockSpec((tm, d), lambda i, lo, hi: (i, 0)),
            pl.BlockSpec((1, d), lambda i, lo, hi: (0, 0)),
        ],
        out_specs=pl.BlockSpec(memory_space=pl.ANY),
        scratch_shapes=[pltpu.VMEM((2, tm, d), F32),
                        pltpu.VMEM((V7X_SUBLANES, d), F32),
                        pltpu.SemaphoreType.DMA((2,)),
                        pltpu.SemaphoreType.DMA(())],
    )
    return pl.pallas_call(
        _dispatch_body,
        grid_spec=grid_spec,
        out_shape=jax.ShapeDtypeStruct((n_slots, d), F32),
        compiler_params=_params(("arbitrary",)),
        name="dispatch",
    )(pad_lo, pad_hi, dest3, x1, g)


def _experts_body(ge_ref, gb_ref, gn_ref, meta_ref, xs_hbm, wg_hbm, wu_hbm, wd_hbm, bg_ref, bu_ref, bd_ref, y_hbm,
                  wfg_ref, wfu_ref, wfd_ref, wbg_ref, wbu_ref, wbd_ref, acc_ref, xsb_ref, stage_ref, zero_ref,
                  w_sems, x_sems, out_sem, zero_sem):
    rows = MOE_BLOCK
    tf = wbg_ref.shape[-1]
    n_chunks = wg_hbm.shape[-1] // tf
    assert n_chunks % 2 == 0 and GROUP_BLOCKS <= n_chunks + 1
    n_groups = meta_ref[0]
    n_used = meta_ref[1]
    n_blocks = y_hbm.shape[0] // rows
    zrows = zero_ref.shape[0]
    last_group = n_groups - 1

    def block_rows(j):
        return pl.ds(pl.multiple_of(j * rows, rows), rows)

    def zero_fill(wait):
        def body(blk, carry):
            for q in range(rows // zrows):
                dst = pl.ds(pl.multiple_of(blk * rows + q * zrows, zrows), zrows)
                cp = pltpu.make_async_copy(zero_ref, y_hbm.at[dst, :], zero_sem)
                if wait:
                    cp.wait()
                else:
                    cp.start()
            return carry
        lax.fori_loop(n_used, n_blocks, body, 0)

    def weight_copies(e, c):
        cols = slice(c * tf, (c + 1) * tf)
        return (pltpu.make_async_copy(wg_hbm.at[e, :, cols], wfg_ref, w_sems.at[0]),
                pltpu.make_async_copy(wu_hbm.at[e, :, cols], wfu_ref, w_sems.at[1]),
                pltpu.make_async_copy(wd_hbm.at[e, cols, :], wfd_ref, w_sems.at[2]))

    def convert_weights(slot):
        wbg_ref[slot] = wfg_ref[...].astype(BF16)
        wbu_ref[slot] = wfu_ref[...].astype(BF16)
        wbd_ref[slot] = wfd_ref[...].astype(BF16)

    def rows_copy(blk, s):
        src = pl.ds(pl.multiple_of(blk * rows, rows), rows)
        return pltpu.make_async_copy(xs_hbm.at[src, :], stage_ref.at[s], x_sems.at[s])

    def out_copy(blk0, j):
        dst = pl.ds(pl.multiple_of((blk0 + j) * rows, rows), rows)
        return pltpu.make_async_copy(acc_ref.at[block_rows(j), :], y_hbm.at[dst, :], out_sem)

    zero_ref[...] = jnp.zeros_like(zero_ref)
    zero_fill(wait=False)
    for cp in weight_copies(ge_ref[0], 0):
        cp.start()

    def first_rows(j, carry):
        cp = rows_copy(gb_ref[0] + j, 0)
        cp.start()
        cp.wait()
        xsb_ref[0, block_rows(j), :] = stage_ref[0].astype(BF16)
        return carry
    lax.fori_loop(0, gn_ref[0], first_rows, 0)
    for cp in weight_copies(ge_ref[0], 0):
        cp.wait()
    convert_weights(0)
    for cp in weight_copies(ge_ref[0], 1):
        cp.start()

    def group_body(g, prev_nblk):
        e = ge_ref[g]
        nblk = gn_ref[g]
        blk0 = gb_ref[g]
        par = g % 2
        g_next = jnp.minimum(g + 1, last_group)
        e_next = ge_ref[g_next]
        blk0_next = gb_ref[g_next]
        nblk_next = jnp.where(g < last_group, gn_ref[g_next], 0)

        def wait_out(j, carry):
            out_copy(0, j).wait()
            return carry
        lax.fori_loop(0, prev_nblk, wait_out, 0)

        for c in range(n_chunks):
            wp = c % 2
            prefetch = [c] + ([n_chunks] if c == n_chunks - 1 else [])
            for s, jb in enumerate(prefetch):
                @pl.when(jb < nblk_next)
                def _():
                    rows_copy(blk0_next + jb, s).start()

            def block(j):
                xs = xsb_ref[par, block_rows(j), :]
                gg = _dot(xs, wbg_ref[wp]) + bg_ref[pl.ds(e, 1), c * tf:(c + 1) * tf]
                uu = _dot(xs, wbu_ref[wp]) + bu_ref[pl.ds(e, 1), c * tf:(c + 1) * tf]
                gg = jnp.minimum(gg, SWIGLU_LIMIT)
                uu = jnp.clip(uu, -SWIGLU_LIMIT, SWIGLU_LIMIT)
                glu = gg * (1.0 / (1.0 + jnp.exp(-SWIGLU_ALPHA * gg)))
                act = ((uu + 1.0) * glu).astype(BF16)
                part = _dot(act, wbd_ref[wp])
                if c == 0:
                    acc_ref[block_rows(j), :] = part + bd_ref[pl.ds(e, 1), :]
                else:
                    acc_ref[block_rows(j), :] += part
                if c == n_chunks - 1:
                    out_copy(blk0, j).start()

            nxt = (e, c + 1) if c + 1 < n_chunks else (e_next, 0)
            for cp in weight_copies(*nxt):
                cp.wait()
            block(0)
            convert_weights(1 - wp)
            nxt2 = (e, c + 2) if c + 2 < n_chunks else (e_next, c + 2 - n_chunks)
            for cp in weight_copies(*nxt2):
                cp.start()

            def rest(j, carry):
                block(j)
                return carry
            lax.fori_loop(1, nblk, rest, 0)

            for s, jb in enumerate(prefetch):
                @pl.when(jb < nblk_next)
                def _():
                    rows_copy(blk0_next + jb, s).wait()
                    xsb_ref[1 - par, block_rows(jb), :] = stage_ref[s].astype(BF16)
        return nblk

    last_nblk = lax.fori_loop(0, n_groups, group_body, 0)

    def wait_out(j, carry):
        out_copy(0, j).wait()
        return carry
    lax.fori_loop(0, last_nblk, wait_out, 0)
    for cp in weight_copies(ge_ref[0], 0):
        cp.wait()
    zero_fill(wait=True)


def _experts(group_e, group_blk, group_n, meta, xs, wg, bg, wu, bu, wd, bd):
    n_slots, d = xs.shape
    dff = wg.shape[-1]
    tf = FF_CHUNK
    smem = pl.BlockSpec(memory_space=pltpu.SMEM)
    hbm = pl.BlockSpec(memory_space=pl.ANY)
    vmem = pl.BlockSpec(memory_space=pltpu.VMEM)
    return pl.pallas_call(
        _experts_body,
        in_specs=[smem, smem, smem, smem, hbm, hbm, hbm, hbm, vmem, vmem, vmem],
        out_specs=hbm,
        out_shape=jax.ShapeDtypeStruct((n_slots, d), F32),
        scratch_shapes=[pltpu.VMEM((d, tf), F32), pltpu.VMEM((d, tf), F32), pltpu.VMEM((tf, d), F32),
                        pltpu.VMEM((2, d, tf), BF16), pltpu.VMEM((2, d, tf), BF16), pltpu.VMEM((2, tf, d), BF16),
                        pltpu.VMEM((GROUP_BLOCKS * MOE_BLOCK, d), F32),
                        pltpu.VMEM((2, GROUP_BLOCKS * MOE_BLOCK, d), BF16),
                        pltpu.VMEM((2, MOE_BLOCK, d), F32),
                        pltpu.VMEM((ZERO_ROWS, d), F32),
                        pltpu.SemaphoreType.DMA((3,)),
                        pltpu.SemaphoreType.DMA((2,)),
                        pltpu.SemaphoreType.DMA(()),
                        pltpu.SemaphoreType.DMA(())],
        compiler_params=pltpu.CompilerParams(vmem_limit_bytes=VMEM_LIMIT_BYTES),
        name="experts",
    )(group_e, group_blk, group_n, meta, xs, wg, wu, wd, bg, bu, bd)


def _combine_body(dest_ref, dest_next_ref, x1_ref, gate_ref, y_hbm, o_ref, buf_ref, sems):
    tm = x1_ref.shape[0]
    i = pl.program_id(0)
    slot = i % 2

    def start_gather(ids_ref, s):
        sub = V7X_SUBLANES

        def body(r0, carry):
            base = pl.multiple_of(r0 * sub, sub)
            for u in range(sub):
                for kk in range(TOP_K):
                    src = ids_ref[0, 0, (base + u) * TOP_K + kk]
                    pltpu.make_async_copy(y_hbm.at[pl.ds(src, 1), :],
                                          buf_ref.at[s, kk, pl.ds(base + u, 1), :], sems.at[s]).start()
            return carry
        lax.fori_loop(0, tm // sub, body, 0)

    @pl.when(i == 0)
    def _():
        start_gather(dest_ref, 0)

    @pl.when(i + 1 < pl.num_programs(0))
    def _():
        start_gather(dest_next_ref, 1 - slot)

    pltpu.make_async_copy(buf_ref.at[slot], buf_ref.at[slot], sems.at[slot]).wait()
    acc = x1_ref[...]
    for kk in range(TOP_K):
        acc = acc + gate_ref[:, kk:kk + 1] * buf_ref[slot, kk]
    o_ref[...] = acc


def _combine(dest, x1, gates, yb):
    n, d = x1.shape
    tm = COMB_ROWS
    n_tiles = n // tm
    dest3 = dest.reshape(n_tiles, 1, tm * TOP_K)
    row = lambda i: (i, 0)
    return pl.pallas_call(
        _combine_body,
        grid=(n_tiles,),
        in_specs=[
            pl.BlockSpec((1, 1, tm * TOP_K), lambda i: (i, 0, 0), memory_space=pltpu.SMEM),
            pl.BlockSpec((1, 1, tm * TOP_K), lambda i: (jnp.minimum(i + 1, n_tiles - 1), 0, 0),
                         memory_space=pltpu.SMEM),
            pl.BlockSpec((tm, d), row),
            pl.BlockSpec((tm, V7X_LANES), row),
            pl.BlockSpec(memory_space=pl.ANY),
        ],
        out_specs=pl.BlockSpec((tm, d), row),
        out_shape=jax.ShapeDtypeStruct((n, d), F32),
        scratch_shapes=[pltpu.VMEM((2, TOP_K, tm, d), F32), pltpu.SemaphoreType.DMA((2,))],
        compiler_params=_params(("arbitrary",)),
        name="combine",
    )(dest3, dest3, x1, gates, yb)


def _slot_layout(top_idx, n_tok):
    nk = n_tok * TOP_K
    onehot = (top_idx[:, :, None] == jnp.arange(N_EXPERTS, dtype=jnp.int32)[None, None, :])
    member = jnp.sum(onehot.astype(jnp.int32), axis=1)
    csum = jnp.cumsum(member, axis=0)
    counts = csum[-1]
    rank = jnp.take_along_axis(csum - member, top_idx, axis=1)
    padded = ((counts + MOE_BLOCK - 1) // MOE_BLOCK) * MOE_BLOCK
    pend = jnp.cumsum(padded)
    pstart = pend - padded
    dest = (pstart[top_idx] + rank).astype(jnp.int32)
    n_blocks = -(-nk // MOE_BLOCK) + N_EXPERTS
    n_slots = n_blocks * MOE_BLOCK
    pad_lo = jnp.concatenate([pstart + counts, pend[-1:]]).astype(jnp.int32)
    pad_hi = jnp.concatenate([pend, jnp.full((1,), n_slots, pend.dtype)]).astype(jnp.int32)
    eblk = (padded // MOE_BLOCK).astype(jnp.int32)
    gcount = (eblk + GROUP_BLOCKS - 1) // GROUP_BLOCKS
    gend = jnp.cumsum(gcount)
    gstart = gend - gcount
    n_groups = (n_blocks + N_EXPERTS * (GROUP_BLOCKS - 1)) // GROUP_BLOCKS
    gid = jnp.arange(n_groups, dtype=jnp.int32)
    valid = gid < gend[-1]
    last_gid = jnp.maximum(gend[-1] - 1, 0)
    gsel = jnp.where(valid, gid, last_gid)
    group_e = jnp.minimum(jnp.sum((gsel[:, None] >= gend[None, :]).astype(jnp.int32), axis=1), N_EXPERTS - 1)
    local = gsel - gstart[group_e]
    group_blk = (pstart[group_e] // MOE_BLOCK + local * GROUP_BLOCKS).astype(jnp.int32)
    group_n = jnp.where(valid, jnp.clip(eblk[group_e] - local * GROUP_BLOCKS, 0, GROUP_BLOCKS), 0).astype(jnp.int32)
    meta = jnp.stack([gend[-1], pend[-1] // MOE_BLOCK]).astype(jnp.int32)
    return dest, n_slots, pad_lo, pad_hi, group_e.astype(jnp.int32), group_blk, group_n, meta


def _pad_lanes(v, lane0=0):
    out = jnp.zeros((1, V7X_LANES), F32)
    return out.at[0, lane0:lane0 + v.shape[0]].set(v.astype(F32))


def kernel(x, attn_norm_g, w_in, forget_bias, q_norm_g, k_norm_g, conv_w, conv_b, dt_bias, a_log, d_skip,
           ssd_norm_g, w_out, ffn_norm_g, w_router, b_router, w_gate, b_gate, w_up, b_up, w_down, b_down):
    b, s, d = x.shape
    depth = attn_norm_g.shape[0]
    att_w = ATT_HEADS * ATT_HEAD_DIM
    ssd_w = w_out.shape[1] - att_w
    ssd_heads = ssd_w // SSD_HEAD_DIM
    conv_ch = ssd_w + 2 * SSD_GROUPS * SSD_STATE
    n_tok = b * s
    dt_lane0 = ATT_HEADS
    o_f = 3 * att_w
    o_z = o_f + ATT_HEADS
    o_xbc = o_z + ssd_w
    o_dt = o_xbc + conv_ch

    x2d = x.reshape(n_tok, d)
    for layer in range(depth):
        wl = w_in[layer]
        w_all = jnp.concatenate(
            [wl[:, :o_f], wl[:, o_z:o_dt], wl[:, o_f:o_z], wl[:, o_dt:o_dt + ssd_heads],
             jnp.zeros((d, SMALL_COLS - ATT_HEADS - ssd_heads), F32)], axis=1).astype(BF16)

        q, k, v, z, xbc, small = _in_proj(
            x2d, attn_norm_g[layer][None, :], w_all, q_norm_g[layer][None, :], k_norm_g[layer][None, :],
            att_w=att_w, ssd_w=ssd_w, conv_ch=conv_ch)

        small3 = small.reshape(b, s, SMALL_COLS)
        cum_s, cum_t = _forget_cum(small3, _pad_lanes(forget_bias[layer]))
        att = _attention(q.reshape(b, s, att_w), k.reshape(b, s, att_w), v.reshape(b, s, att_w), cum_s, cum_t)

        ssd = _ssd(xbc.reshape(b, s, conv_ch), z.reshape(b, s, ssd_w), small3,
                   conv_w[layer], conv_b[layer][None, :],
                   _pad_lanes(dt_bias[layer], dt_lane0), _pad_lanes(a_log[layer], dt_lane0),
                   jnp.repeat(d_skip[layer], SSD_HEAD_DIM)[None, :], ssd_norm_g[layer][None, :],
                   dt_lane0=dt_lane0)

        wr = jnp.zeros((d, V7X_LANES), F32).at[:, :N_EXPERTS].set(w_router[layer])
        wr_hi = wr.astype(BF16)
        wr_lo = (wr - wr_hi.astype(F32)).astype(BF16)
        x1, idx_pad, gate_pad = _out_proj(
            att.reshape(n_tok, att_w), ssd.reshape(n_tok, ssd_w), x2d, w_out[layer].astype(BF16),
            ffn_norm_g[layer][None, :], wr_hi, wr_lo, _pad_lanes(b_router[layer]))

        dest, n_slots, pad_lo, pad_hi, group_e, group_blk, group_n, meta = _slot_layout(idx_pad[:, :TOP_K], n_tok)
        dest = dest.reshape(-1)
        xs = _dispatch(pad_lo, pad_hi, dest, x1, ffn_norm_g[layer][None, :], n_slots)
        yb = _experts(group_e, group_blk, group_n, meta, xs,
                      w_gate[layer], b_gate[layer], w_up[layer], b_up[layer], w_down[layer], b_down[layer])
        x2d = _combine(dest, x1, gate_pad, yb)
    return x2d.reshape(b, s, d)
```

```python
import functools
import math

import jax
import jax.numpy as jnp
from jax import lax
from jax.experimental import pallas as pl
from jax.experimental.pallas import tpu as pltpu

F32 = jnp.float32
BF16 = jnp.bfloat16

ATT_HEADS = 8
ATT_HEAD_DIM = 128
SSD_HEAD_DIM = 64
SSD_GROUPS = 2
SSD_STATE = 128
CONV_WIDTH = 4
N_EXPERTS = 32
TOP_K = 4
SWIGLU_ALPHA = 1.702
SWIGLU_LIMIT = 7.0
EPS = 1e-6

V7X_LANES = 128
V7X_SUBLANES = 8
V7X_VMEM_BYTES = 64 * 1024 * 1024
VMEM_LIMIT_BYTES = 58 * 1024 * 1024

PROJ_ROWS = 256
OUT_ROWS = 512
PACK_ROWS = 256
PROJ_COLS = 512
CUM_ROWS = 256
ATT_BLOCK = 512
SSD_ROWS = 256
DISP_ROWS = 256
MOE_BLOCK = 256
GROUP_BLOCKS = 5
ZERO_ROWS = 32
FF_CHUNK = 512
COMB_ROWS = 128
SMALL_COLS = V7X_LANES


def _params(dims):
    return pltpu.CompilerParams(dimension_semantics=dims, vmem_limit_bytes=VMEM_LIMIT_BYTES)


def _rms(x, g, eps):
    return x * lax.rsqrt(jnp.mean(x * x, axis=-1, keepdims=True) + eps) * g


def _dot(a, b):
    return jnp.dot(a, b, preferred_element_type=F32)


def _cumsum_rows(x):
    n = x.shape[0]
    row = lax.broadcasted_iota(jnp.int32, x.shape, 0)
    step = 1
    while step < n:
        x = x + jnp.where(row >= step, pltpu.roll(x, step, axis=0), 0.0)
        step *= 2
    return x


def _softplus(x):
    return jnp.maximum(x, 0.0) + jnp.log(1.0 + jnp.exp(-jnp.abs(x)))


def _silu(x):
    return x * (1.0 / (1.0 + jnp.exp(-x)))


def _w_pack_body(w_ref, o_ref, *, o_f, o_z, o_dt, n_dt):
    w = w_ref[0]
    n_f = o_z - o_f
    main = o_f + (o_dt - o_z)
    o_ref[:, 0:o_f] = w[:, 0:o_f].astype(o_ref.dtype)
    o_ref[:, o_f:main] = w[:, o_z:o_dt].astype(o_ref.dtype)
    small = jnp.concatenate(
        [w[:, o_f:o_z], w[:, o_dt:o_dt + n_dt], jnp.zeros((w.shape[0], SMALL_COLS - n_f - n_dt), w.dtype)], axis=1)
    o_ref[:, main:main + SMALL_COLS] = small.astype(o_ref.dtype)


def _w_pack(w_in, layer, *, o_f, o_z, o_dt, n_dt):
    _, d, cols = w_in.shape
    out_cols = o_f + (o_dt - o_z) + SMALL_COLS
    return pl.pallas_call(
        functools.partial(_w_pack_body, o_f=o_f, o_z=o_z, o_dt=o_dt, n_dt=n_dt),
        grid=(d // PACK_ROWS,),
        in_specs=[pl.BlockSpec((1, PACK_ROWS, cols), lambda i: (layer, i, 0))],
        out_specs=pl.BlockSpec((PACK_ROWS, out_cols), lambda i: (i, 0)),
        out_shape=jax.ShapeDtypeStruct((d, out_cols), BF16),
        compiler_params=_params(("arbitrary",)),
        name="w_pack",
    )(w_in)


def _in_proj_body(x_ref, g_ref, w_ref, qg_ref, kg_ref,
                  q_ref, k_ref, v_ref, z_ref, xbc_ref, sm_ref, *, head_dim, q_scale):
    att_w = q_ref.shape[-1]
    ssd_w = z_ref.shape[-1]
    conv_ch = xbc_ref.shape[-1]
    h = _rms(x_ref[...], g_ref[...], EPS).astype(BF16)

    def proj(c0, width):
        return _dot(h, w_ref[:, c0:c0 + width])

    def chunks(width):
        return [(c, min(PROJ_COLS, width - c)) for c in range(0, width, PROJ_COLS)]

    col = 0
    for out_ref, gain_ref, scale in ((q_ref, qg_ref, q_scale), (k_ref, kg_ref, None)):
        for c0, cw in chunks(att_w):
            y = proj(col + c0, cw)
            for j in range(cw // head_dim):
                yn = _rms(y[:, j * head_dim:(j + 1) * head_dim], gain_ref[...], EPS)
                if scale is not None:
                    yn = yn * scale
                out_ref[:, c0 + j * head_dim:c0 + (j + 1) * head_dim] = yn.astype(out_ref.dtype)
        col += att_w
    for out_ref, width in ((v_ref, att_w), (z_ref, ssd_w), (xbc_ref, conv_ch), (sm_ref, SMALL_COLS)):
        for c0, cw in chunks(width):
            out_ref[:, c0:c0 + cw] = proj(col + c0, cw).astype(out_ref.dtype)
        col += width


def _in_proj(x2d, g, w_all, qg, kg, *, att_w, ssd_w, conv_ch):
    n, d = x2d.shape
    tm = PROJ_ROWS
    row = lambda i: (i, 0)
    fixed = lambda i: (0, 0)
    outs = [(att_w, BF16), (att_w, BF16), (att_w, BF16), (ssd_w, F32), (conv_ch, F32), (SMALL_COLS, F32)]
    return pl.pallas_call(
        functools.partial(_in_proj_body, head_dim=ATT_HEAD_DIM, q_scale=1.0 / math.sqrt(ATT_HEAD_DIM)),
        grid=(n // tm,),
        in_specs=[
            pl.BlockSpec((tm, d), row),
            pl.BlockSpec((1, d), fixed),
            pl.BlockSpec(w_all.shape, fixed, pipeline_mode=pl.Buffered(1)),
            pl.BlockSpec((1, ATT_HEAD_DIM), fixed),
            pl.BlockSpec((1, ATT_HEAD_DIM), fixed),
        ],
        out_specs=[pl.BlockSpec((tm, w), row) for w, _ in outs],
        out_shape=[jax.ShapeDtypeStruct((n, w), dt) for w, dt in outs],
        compiler_params=_params(("arbitrary",)),
        name="in_proj",
    )(x2d, g, w_all, qg, kg)


def _forget_cum_body(sm_ref, fb_ref, cs_ref, ct_ref, carry_ref):
    @pl.when(pl.program_id(1) == 0)
    def _():
        carry_ref[...] = jnp.zeros_like(carry_ref)

    n = sm_ref.shape[1]
    log_f = -_softplus(-(sm_ref[0] + fb_ref[...]))
    c = _cumsum_rows(log_f) + carry_ref[...]
    cs_ref[0] = c
    ct_ref[0] = c.T[0:ct_ref.shape[1], :]
    carry_ref[...] = c[n - 1:n, :]


def _forget_cum(small, fb):
    b, s, w = small.shape
    tc = CUM_ROWS
    return pl.pallas_call(
        _forget_cum_body,
        grid=(b, s // tc),
        in_specs=[pl.BlockSpec((1, tc, w), lambda i, j: (i, j, 0)),
                  pl.BlockSpec((1, w), lambda i, j: (0, 0))],
        out_specs=[pl.BlockSpec((1, tc, w), lambda i, j: (i, j, 0)),
                   pl.BlockSpec((1, V7X_SUBLANES, tc), lambda i, j: (i, 0, j))],
        out_shape=[jax.ShapeDtypeStruct((b, s, w), F32),
                   jax.ShapeDtypeStruct((b, V7X_SUBLANES, s), F32)],
        scratch_shapes=[pltpu.VMEM((1, w), F32)],
        compiler_params=_params(("arbitrary", "arbitrary")),
        name="forget_cum",
    )(small, fb)


def _attn_body(q_ref, k_ref, v_ref, cs_ref, ct_ref, o_ref):
    head = pl.program_id(1)
    s_len = q_ref.shape[1]
    t = ATT_BLOCK
    lane = lax.broadcasted_iota(jnp.int32, (t, cs_ref.shape[-1]), 1)
    row = lax.broadcasted_iota(jnp.int32, (t, t), 0)
    col = lax.broadcasted_iota(jnp.int32, (t, t), 1)
    causal = col <= row
    for qi in range(s_len // t):
        rows = slice(qi * t, (qi + 1) * t)
        q = q_ref[0, rows, :]
        cq = jnp.sum(jnp.where(lane == head, cs_ref[0, rows, :], 0.0), axis=-1, keepdims=True)
        m = l = acc = None
        for ki in range(qi + 1):
            cols = slice(ki * t, (ki + 1) * t)
            s = lax.dot_general(q, k_ref[0, cols, :], (((1,), (1,)), ((), ())), preferred_element_type=F32)
            s = (s + cq) - ct_ref[0, pl.ds(head, 1), cols]
            if ki == qi:
                s = jnp.where(causal, s, -jnp.inf)
            m_blk = jnp.max(s, axis=-1, keepdims=True)
            if ki == 0:
                m = m_blk
                pmat = jnp.exp(s - m)
                l = jnp.sum(pmat, axis=-1, keepdims=True)
                acc = _dot(pmat.astype(BF16), v_ref[0, cols, :])
            else:
                m_new = jnp.maximum(m, m_blk)
                alpha = jnp.exp(m - m_new)
                pmat = jnp.exp(s - m_new)
                l = alpha * l + jnp.sum(pmat, axis=-1, keepdims=True)
                acc = alpha * acc + _dot(pmat.astype(BF16), v_ref[0, cols, :])
                m = m_new
        o_ref[0, rows, :] = (acc / l).astype(o_ref.dtype)


def _attention(q, k, v, cum_s, cum_t):
    b, s, w = q.shape
    hd = ATT_HEAD_DIM
    heads = w // hd
    head_blk = pl.BlockSpec((1, s, hd), lambda bi, h: (bi, 0, h))
    return pl.pallas_call(
        _attn_body,
        grid=(b, heads),
        in_specs=[head_blk, head_blk, head_blk,
                  pl.BlockSpec((1, s, cum_s.shape[-1]), lambda bi, h: (bi, 0, 0)),
                  pl.BlockSpec((1, cum_t.shape[1], s), lambda bi, h: (bi, 0, 0))],
        out_specs=head_blk,
        out_shape=jax.ShapeDtypeStruct((b, s, w), BF16),
        compiler_params=_params(("arbitrary", "arbitrary")),
        name="attention",
    )(q, k, v, cum_s, cum_t)


def _ssd_body(xbc_ref, z_ref, sm_ref, cw_ref, cb_ref, dtb_ref, alog_ref, dskip_ref, ng_ref,
              o_ref, ext_ref, state_ref, y_ref, *, dt_lane0):
    t = xbc_ref.shape[1]
    width = z_ref.shape[-1]
    lanes = V7X_LANES
    n_pairs = width // lanes
    pairs_per_group = n_pairs // SSD_GROUPS
    tail = V7X_SUBLANES

    @pl.when(pl.program_id(1) == 0)
    def _():
        ext_ref[...] = jnp.zeros_like(ext_ref)
        state_ref[...] = jnp.zeros_like(state_ref)

    xb = xbc_ref[0]
    prev = ext_ref[...]
    row8 = lax.broadcasted_iota(jnp.int32, prev.shape, 0)
    conv = cb_ref[...] + xb * cw_ref[CONV_WIDTH - 1:CONV_WIDTH, :]
    for k in range(1, CONV_WIDTH):
        rolled = pltpu.roll(xb, k, axis=0)
        head = jnp.where(row8 < k, pltpu.roll(prev, k, axis=0), rolled[0:tail, :])
        shifted = jnp.concatenate([head, rolled[tail:, :]], axis=0)
        conv = conv + shifted * cw_ref[CONV_WIDTH - 1 - k:CONV_WIDTH - k, :]
    ext_ref[...] = xb[t - tail:t, :]
    conv = _silu(conv)

    sm = sm_ref[0]
    dt = _softplus(sm + dtb_ref[...])
    d_a = dt * (-jnp.exp(alog_ref[...]))
    a_cs = _cumsum_rows(d_a)
    a_cs_t = a_cs.T
    a_last = a_cs[t - 1:t, :]

    row = lax.broadcasted_iota(jnp.int32, (t, t), 0)
    col = lax.broadcasted_iota(jnp.int32, (t, t), 1)
    causal = row >= col
    lane = lax.broadcasted_iota(jnp.int32, (t, lanes), 1)
    first_half = lane < SSD_HEAD_DIM
    lane1 = lax.broadcasted_iota(jnp.int32, (1, lanes), 1)

    b_off = width
    c_off = width + SSD_GROUPS * SSD_STATE
    sumsq = jnp.zeros((t, 1), F32)
    for g in range(SSD_GROUPS):
        b_g = conv[:, b_off + g * SSD_STATE:b_off + (g + 1) * SSD_STATE].astype(BF16)
        c_g = conv[:, c_off + g * SSD_STATE:c_off + (g + 1) * SSD_STATE].astype(BF16)
        cb = lax.dot_general(c_g, b_g, (((1,), (1,)), ((), ())), preferred_element_type=F32)
        for jp in range(pairs_per_group):
            pr = g * pairs_per_group + jp
            h0 = dt_lane0 + 2 * pr
            xs = conv[:, pr * lanes:(pr + 1) * lanes]
            a_pair = jnp.where(first_half, a_cs[:, h0:h0 + 1], a_cs[:, h0 + 1:h0 + 2])
            dt_pair = jnp.where(first_half, dt[:, h0:h0 + 1], dt[:, h0 + 1:h0 + 2])
            a_last_pair = jnp.where(lane1 < SSD_HEAD_DIM, a_last[:, h0:h0 + 1], a_last[:, h0 + 1:h0 + 2])
            x_dt = xs * dt_pair
            y = jnp.zeros((t, lanes), F32)
            for half in range(2):
                hh = h0 + half
                diff = a_cs[:, hh:hh + 1] - a_cs_t[hh:hh + 1, :]
                wmat = cb * jnp.exp(jnp.where(causal, diff, -jnp.inf))
                keep = first_half if half == 0 else jnp.logical_not(first_half)
                y = y + _dot(wmat.astype(BF16), jnp.where(keep, x_dt, 0.0).astype(BF16))
            st = state_ref[pr]
            y = y + _dot(c_g, st.astype(BF16)) * jnp.exp(a_pair)
            x_dec = (x_dt * jnp.exp(a_last_pair - a_pair)).astype(BF16)
            upd = lax.dot_general(b_g, x_dec, (((0,), (0,)), ((), ())), preferred_element_type=F32)
            state_ref[pr] = st * jnp.exp(a_last_pair) + upd
            y = y + xs * dskip_ref[:, pr * lanes:(pr + 1) * lanes]
            zz = z_ref[0, :, pr * lanes:(pr + 1) * lanes]
            y = y * _silu(zz)
            sumsq = sumsq + jnp.sum(y * y, axis=-1, keepdims=True)
            y_ref[:, pr * lanes:(pr + 1) * lanes] = y
    scale = lax.rsqrt(sumsq / width + EPS)
    o_ref[0] = (y_ref[...] * scale * ng_ref[...]).astype(o_ref.dtype)


def _ssd(xbc, z, small, conv_w, conv_b, dtb, alog, dskip, norm_g, *, dt_lane0):
    b, s, cc = xbc.shape
    width = z.shape[-1]
    t = SSD_ROWS
    n_pairs = width // V7X_LANES
    blk = lambda i, j: (i, j, 0)
    fixed = lambda i, j: (0, 0)
    return pl.pallas_call(
        functools.partial(_ssd_body, dt_lane0=dt_lane0),
        grid=(b, s // t),
        in_specs=[
            pl.BlockSpec((1, t, cc), blk),
            pl.BlockSpec((1, t, width), blk),
            pl.BlockSpec((1, t, small.shape[-1]), blk),
            pl.BlockSpec(conv_w.shape, fixed),
            pl.BlockSpec(conv_b.shape, fixed),
            pl.BlockSpec(dtb.shape, fixed),
            pl.BlockSpec(alog.shape, fixed),
            pl.BlockSpec(dskip.shape, fixed),
            pl.BlockSpec(norm_g.shape, fixed),
        ],
        out_specs=pl.BlockSpec((1, t, width), blk),
        out_shape=jax.ShapeDtypeStruct((b, s, width), BF16),
        scratch_shapes=[
            pltpu.VMEM((V7X_SUBLANES, cc), F32),
            pltpu.VMEM((n_pairs, SSD_STATE, V7X_LANES), F32),
            pltpu.VMEM((t, width), F32),
        ],
        compiler_params=_params(("arbitrary", "arbitrary")),
        name="ssd",
    )(xbc, z, small, conv_w, conv_b, dtb, alog, dskip, norm_g)


def _out_proj_body(att_ref, ssd_ref, x_ref, wo_ref, g_ref, wrh_ref, wrl_ref, br_ref,
                   x1_ref, idx_ref, gate_ref):
    aw = att_ref.shape[-1]
    d = x_ref.shape[-1]
    tm = x_ref.shape[0]
    for c0 in range(0, d, PROJ_COLS):
        y = _dot(att_ref[...], wo_ref[0:aw, c0:c0 + PROJ_COLS])
        y = y + _dot(ssd_ref[...], wo_ref[aw:, c0:c0 + PROJ_COLS])
        x1_ref[:, c0:c0 + PROJ_COLS] = x_ref[:, c0:c0 + PROJ_COLS] + y
    h2 = _rms(x1_ref[...], g_ref[...], EPS)
    h_hi = h2.astype(BF16)
    h_lo = (h2 - h_hi.astype(F32)).astype(BF16)
    logits = _dot(h_hi, wrh_ref[...]) + _dot(h_lo, wrh_ref[...]) + _dot(h_hi, wrl_ref[...]) + br_ref[...]
    lane = lax.broadcasted_iota(jnp.int32, (tm, V7X_LANES), 1)
    logits = jnp.where(lane < N_EXPERTS, logits, -jnp.inf)
    lane_f = lane.astype(F32)
    idx_out = jnp.zeros((tm, V7X_LANES), F32)
    vals = []
    for kk in range(TOP_K):
        m = jnp.max(logits, axis=-1, keepdims=True)
        sel = jnp.min(jnp.where(logits == m, lane_f, float(V7X_LANES)), axis=-1, keepdims=True)
        idx_out = jnp.where(lane == kk, sel, idx_out)
        logits = jnp.where(lane_f == sel, -jnp.inf, logits)
        vals.append(m)
    exps = [jnp.exp(v - vals[0]) for v in vals]
    denom = exps[0]
    for e in exps[1:]:
        denom = denom + e
    gate_out = jnp.zeros((tm, V7X_LANES), F32)
    for kk in range(TOP_K):
        gate_out = jnp.where(lane == kk, exps[kk] / denom, gate_out)
    idx_ref[...] = idx_out.astype(jnp.int32)
    gate_ref[...] = gate_out


def _out_proj(att, ssd, x2d, wo, g, wr_hi, wr_lo, br):
    n, d = x2d.shape
    tm = OUT_ROWS
    row = lambda i: (i, 0)
    fixed = lambda i: (0, 0)
    return pl.pallas_call(
        _out_proj_body,
        grid=(n // tm,),
        in_specs=[
            pl.BlockSpec((tm, att.shape[-1]), row),
            pl.BlockSpec((tm, ssd.shape[-1]), row),
            pl.BlockSpec((tm, d), row),
            pl.BlockSpec(wo.shape, fixed, pipeline_mode=pl.Buffered(1)),
            pl.BlockSpec((1, d), fixed),
            pl.BlockSpec(wr_hi.shape, fixed),
            pl.BlockSpec(wr_lo.shape, fixed),
            pl.BlockSpec((1, V7X_LANES), fixed),
        ],
        out_specs=[pl.BlockSpec((tm, d), row),
                   pl.BlockSpec((tm, V7X_LANES), row),
                   pl.BlockSpec((tm, V7X_LANES), row)],
        out_shape=[jax.ShapeDtypeStruct((n, d), F32),
                   jax.ShapeDtypeStruct((n, V7X_LANES), jnp.int32),
                   jax.ShapeDtypeStruct((n, V7X_LANES), F32)],
        compiler_params=_params(("arbitrary",)),
        name="out_proj",
    )(att, ssd, x2d, wo, g, wr_hi, wr_lo, br)


def _dispatch_body(plo_ref, phi_ref, dest_ref, x1_ref, g_ref, xs_hbm, stage_ref, zero_ref, sems, zero_sem):
    i = pl.program_id(0)
    n = pl.num_programs(0)
    slot = i % 2
    tm = x1_ref.shape[0]
    sub = V7X_SUBLANES

    def drain(s):
        for _ in range(TOP_K):
            pltpu.make_async_copy(stage_ref.at[s], stage_ref.at[s], sems.at[s]).wait()

    @pl.when(i >= 2)
    def _():
        drain(slot)

    stage_ref[slot] = _rms(x1_ref[...], g_ref[...], EPS)

    def body(r0, carry):
        base = pl.multiple_of(r0 * sub, sub)
        for u in range(sub):
            for kk in range(TOP_K):
                dst = dest_ref[0, 0, (base + u) * TOP_K + kk]
                pltpu.make_async_copy(stage_ref.at[slot, pl.ds(base + u, 1), :],
                                      xs_hbm.at[pl.ds(dst, 1), :], sems.at[slot]).start()
        return carry
    lax.fori_loop(0, tm // sub, body, 0)

    def pad_fill(wait):
        def go(cp):
            if wait:
                cp.wait()
            else:
                cp.start()

        def per_expert(e, carry):
            lo = plo_ref[e]
            hi = phi_ref[e]
            aligned = jnp.minimum(hi, lax.shift_left(lax.shift_right_logical(lo + (sub - 1), 3), 3))

            def one(s, c2):
                go(pltpu.make_async_copy(zero_ref.at[pl.ds(0, 1), :], xs_hbm.at[pl.ds(s, 1), :], zero_sem))
                return c2
            lax.fori_loop(lo, aligned, one, 0)

            def eight(q, c2):
                s8 = pl.multiple_of(aligned + q * sub, sub)
                go(pltpu.make_async_copy(zero_ref, xs_hbm.at[pl.ds(s8, sub), :], zero_sem))
                return c2
            lax.fori_loop(0, lax.shift_right_logical(hi - aligned, 3), eight, 0)
            return carry
        lax.fori_loop(0, plo_ref.shape[0], per_expert, 0)

    @pl.when(i == n - 1)
    def _():
        zero_ref[...] = jnp.zeros_like(zero_ref)
        pad_fill(wait=False)
        drain(slot)
        drain(1 - slot)
        pad_fill(wait=True)


def _dispatch(pad_lo, pad_hi, dest, x1, g, n_slots):
    n, d = x1.shape
    tm = DISP_ROWS
    n_tiles = n // tm
    assert n_tiles >= 2 and V7X_SUBLANES == 8
    dest3 = dest.reshape(n_tiles, 1, tm * TOP_K)
    grid_spec = pltpu.PrefetchScalarGridSpec(
        num_scalar_prefetch=2,
        grid=(n_tiles,),
        in_specs=[
            pl.BlockSpec((1, 1, tm * TOP_K), lambda i, lo, hi: (i, 0, 0), memory_space=pltpu.SMEM),
            pl.BlockSpec((tm, d), lambda i, lo, hi: (i, 0)),
            pl.BlockSpec((1, d), lambda i, lo, hi: (0, 0)),
        ],
        out_specs=pl.BlockSpec(memory_space=pl.ANY),
        scratch_shapes=[pltpu.VMEM((2, tm, d), F32),
                        pltpu.VMEM((V7X_SUBLANES, d), F32),
                        pltpu.SemaphoreType.DMA((2,)),
                        pltpu.SemaphoreType.DMA(())],
    )
    return pl.pallas_call(
        _dispatch_body,
        grid_spec=grid_spec,
        out_shape=jax.ShapeDtypeStruct((n_slots, d), F32),
        compiler_params=_params(("arbitrary",)),
        name="dispatch",
    )(pad_lo, pad_hi, dest3, x1, g)


def _experts_body(ge_ref, gb_ref, gn_ref, meta_ref, xs_hbm, wg_hbm, wu_hbm, wd_hbm, bg_ref, bu_ref, bd_ref, y_hbm,
                  wfg_ref, wfu_ref, wfd_ref, wbg_ref, wbu_ref, wbd_ref, acc_ref, xsb_ref, stage_ref, zero_ref,
                  w_sems, x_sems, out_sem, zero_sem):
    rows = MOE_BLOCK
    tf = wbg_ref.shape[-1]
    n_chunks = wg_hbm.shape[-1] // tf
    assert n_chunks % 2 == 0 and GROUP_BLOCKS <= n_chunks + 1
    n_groups = meta_ref[0]
    n_used = meta_ref[1]
    n_blocks = y_hbm.shape[0] // rows
    zrows = zero_ref.shape[0]
    last_group = n_groups - 1

    def block_rows(j):
        return pl.ds(pl.multiple_of(j * rows, rows), rows)

    def zero_fill(wait):
        def body(blk, carry):
            for q in range(rows // zrows):
                dst = pl.ds(pl.multiple_of(blk * rows + q * zrows, zrows), zrows)
                cp = pltpu.make_async_copy(zero_ref, y_hbm.at[dst, :], zero_sem)
                if wait:
                    cp.wait()
                else:
                    cp.start()
            return carry
        lax.fori_loop(n_used, n_blocks, body, 0)

    def weight_copies(e, c):
        cols = slice(c * tf, (c + 1) * tf)
        return (pltpu.make_async_copy(wg_hbm.at[e, :, cols], wfg_ref, w_sems.at[0]),
                pltpu.make_async_copy(wu_hbm.at[e, :, cols], wfu_ref, w_sems.at[1]),
                pltpu.make_async_copy(wd_hbm.at[e, cols, :], wfd_ref, w_sems.at[2]))

    def convert_weights(slot):
        wbg_ref[slot] = wfg_ref[...].astype(BF16)
        wbu_ref[slot] = wfu_ref[...].astype(BF16)
        wbd_ref[slot] = wfd_ref[...].astype(BF16)

    def rows_copy(blk, s):
        src = pl.ds(pl.multiple_of(blk * rows, rows), rows)
        return pltpu.make_async_copy(xs_hbm.at[src, :], stage_ref.at[s], x_sems.at[s])

    def out_copy(blk0, j):
        dst = pl.ds(pl.multiple_of((blk0 + j) * rows, rows), rows)
        return pltpu.make_async_copy(acc_ref.at[block_rows(j), :], y_hbm.at[dst, :], out_sem)

    zero_ref[...] = jnp.zeros_like(zero_ref)
    zero_fill(wait=False)
    for cp in weight_copies(ge_ref[0], 0):
        cp.start()

    def first_rows(j, carry):
        cp = rows_copy(gb_ref[0] + j, 0)
        cp.start()
        cp.wait()
        xsb_ref[0, block_rows(j), :] = stage_ref[0].astype(BF16)
        return carry
    lax.fori_loop(0, gn_ref[0], first_rows, 0)
    for cp in weight_copies(ge_ref[0], 0):
        cp.wait()
    convert_weights(0)
    for cp in weight_copies(ge_ref[0], 1):
        cp.start()

    def group_body(g, prev_nblk):
        e = ge_ref[g]
        nblk = gn_ref[g]
        blk0 = gb_ref[g]
        par = g % 2
        g_next = jnp.minimum(g + 1, last_group)
        e_next = ge_ref[g_next]
        blk0_next = gb_ref[g_next]
        nblk_next = jnp.where(g < last_group, gn_ref[g_next], 0)

        def wait_out(j, carry):
            out_copy(0, j).wait()
            return carry
        lax.fori_loop(0, prev_nblk, wait_out, 0)

        for c in range(n_chunks):
            wp = c % 2
            prefetch = [c] + ([n_chunks] if c == n_chunks - 1 else [])
            for s, jb in enumerate(prefetch):
                @pl.when(jb < nblk_next)
                def _():
                    rows_copy(blk0_next + jb, s).start()

            def block(j):
                xs = xsb_ref[par, block_rows(j), :]
                gg = _dot(xs, wbg_ref[wp]) + bg_ref[pl.ds(e, 1), c * tf:(c + 1) * tf]
                uu = _dot(xs, wbu_ref[wp]) + bu_ref[pl.ds(e, 1), c * tf:(c + 1) * tf]
                gg = jnp.minimum(gg, SWIGLU_LIMIT)
                uu = jnp.clip(uu, -SWIGLU_LIMIT, SWIGLU_LIMIT)
                glu = gg * (1.0 / (1.0 + jnp.exp(-SWIGLU_ALPHA * gg)))
                act = ((uu + 1.0) * glu).astype(BF16)
                part = _dot(act, wbd_ref[wp])
                if c == 0:
                    acc_ref[block_rows(j), :] = part + bd_ref[pl.ds(e, 1), :]
                else:
                    acc_ref[block_rows(j), :] += part
                if c == n_chunks - 1:
                    out_copy(blk0, j).start()

            nxt = (e, c + 1) if c + 1 < n_chunks else (e_next, 0)
            for cp in weight_copies(*nxt):
                cp.wait()
            block(0)
            convert_weights(1 - wp)
            nxt2 = (e, c + 2) if c + 2 < n_chunks else (e_next, c + 2 - n_chunks)
            for cp in weight_copies(*nxt2):
                cp.start()

            def pair(i, carry):
                block(1 + 2 * i)
                block(2 + 2 * i)
                return carry
            n_pairs = lax.shift_right_logical(nblk - 1, 1)
            lax.fori_loop(0, n_pairs, pair, 0)

            @pl.when(2 * n_pairs + 1 < nblk)
            def _():
                block(nblk - 1)

            for s, jb in enumerate(prefetch):
                @pl.when(jb < nblk_next)
                def _():
                    rows_copy(blk0_next + jb, s).wait()
                    xsb_ref[1 - par, block_rows(jb), :] = stage_ref[s].astype(BF16)
        return nblk

    last_nblk = lax.fori_loop(0, n_groups, group_body, 0)

    def wait_out(j, carry):
        out_copy(0, j).wait()
        return carry
    lax.fori_loop(0, last_nblk, wait_out, 0)
    for cp in weight_copies(ge_ref[0], 0):
        cp.wait()
    zero_fill(wait=True)


def _experts(group_e, group_blk, group_n, meta, xs, wg, bg, wu, bu, wd, bd):
    n_slots, d = xs.shape
    dff = wg.shape[-1]
    tf = FF_CHUNK
    smem = pl.BlockSpec(memory_space=pltpu.SMEM)
    hbm = pl.BlockSpec(memory_space=pl.ANY)
    vmem = pl.BlockSpec(memory_space=pltpu.VMEM)
    return pl.pallas_call(
        _experts_body,
        in_specs=[smem, smem, smem, smem, hbm, hbm, hbm, hbm, vmem, vmem, vmem],
        out_specs=hbm,
        out_shape=jax.ShapeDtypeStruct((n_slots, d), F32),
        scratch_shapes=[pltpu.VMEM((d, tf), F32), pltpu.VMEM((d, tf), F32), pltpu.VMEM((tf, d), F32),
                        pltpu.VMEM((2, d, tf), BF16), pltpu.VMEM((2, d, tf), BF16), pltpu.VMEM((2, tf, d), BF16),
                        pltpu.VMEM((GROUP_BLOCKS * MOE_BLOCK, d), F32),
                        pltpu.VMEM((2, GROUP_BLOCKS * MOE_BLOCK, d), BF16),
                        pltpu.VMEM((2, MOE_BLOCK, d), F32),
                        pltpu.VMEM((ZERO_ROWS, d), F32),
                        pltpu.SemaphoreType.DMA((3,)),
                        pltpu.SemaphoreType.DMA((2,)),
                        pltpu.SemaphoreType.DMA(()),
                        pltpu.SemaphoreType.DMA(())],
        compiler_params=pltpu.CompilerParams(vmem_limit_bytes=VMEM_LIMIT_BYTES),
        name="experts",
    )(group_e, group_blk, group_n, meta, xs, wg, wu, wd, bg, bu, bd)


def _combine_body(dest_ref, dest_next_ref, x1_ref, gate_ref, y_hbm, o_ref, buf_ref, sems):
    tm = x1_ref.shape[0]
    i = pl.program_id(0)
    slot = i % 2

    def start_gather(ids_ref, s):
        sub = V7X_SUBLANES

        def body(r0, carry):
            base = pl.multiple_of(r0 * sub, sub)
            for u in range(sub):
                for kk in range(TOP_K):
                    src = ids_ref[0, 0, (base + u) * TOP_K + kk]
                    pltpu.make_async_copy(y_hbm.at[pl.ds(src, 1), :],
                                          buf_ref.at[s, kk, pl.ds(base + u, 1), :], sems.at[s]).start()
            return carry
        lax.fori_loop(0, tm // sub, body, 0)

    @pl.when(i == 0)
    def _():
        start_gather(dest_ref, 0)

    @pl.when(i + 1 < pl.num_programs(0))
    def _():
        start_gather(dest_next_ref, 1 - slot)

    pltpu.make_async_copy(buf_ref.at[slot], buf_ref.at[slot], sems.at[slot]).wait()
    acc = x1_ref[...]
    for kk in range(TOP_K):
        acc = acc + gate_ref[:, kk:kk + 1] * buf_ref[slot, kk]
    o_ref[...] = acc


def _combine(dest, x1, gates, yb):
    n, d = x1.shape
    tm = COMB_ROWS
    n_tiles = n // tm
    dest3 = dest.reshape(n_tiles, 1, tm * TOP_K)
    row = lambda i: (i, 0)
    return pl.pallas_call(
        _combine_body,
        grid=(n_tiles,),
        in_specs=[
            pl.BlockSpec((1, 1, tm * TOP_K), lambda i: (i, 0, 0), memory_space=pltpu.SMEM),
            pl.BlockSpec((1, 1, tm * TOP_K), lambda i: (jnp.minimum(i + 1, n_tiles - 1), 0, 0),
                         memory_space=pltpu.SMEM),
            pl.BlockSpec((tm, d), row),
            pl.BlockSpec((tm, V7X_LANES), row),
            pl.BlockSpec(memory_space=pl.ANY),
        ],
        out_specs=pl.BlockSpec((tm, d), row),
        out_shape=jax.ShapeDtypeStruct((n, d), F32),
        scratch_shapes=[pltpu.VMEM((2, TOP_K, tm, d), F32), pltpu.SemaphoreType.DMA((2,))],
        compiler_params=_params(("arbitrary",)),
        name="combine",
    )(dest3, dest3, x1, gates, yb)


def _slot_layout(top_idx, n_tok):
    nk = n_tok * TOP_K
    onehot = (top_idx[:, :, None] == jnp.arange(N_EXPERTS, dtype=jnp.int32)[None, None, :])
    member = jnp.sum(onehot.astype(jnp.int32), axis=1)
    csum = jnp.cumsum(member, axis=0)
    counts = csum[-1]
    padded = ((counts + MOE_BLOCK - 1) // MOE_BLOCK) * MOE_BLOCK
    pend = jnp.cumsum(padded)
    pstart = pend - padded
    slot_te = (csum - member) + pstart[None, :]
    dest = jnp.sum(jnp.where(onehot, slot_te[:, None, :], 0), axis=-1).astype(jnp.int32)
    n_blocks = -(-nk // MOE_BLOCK) + N_EXPERTS
    n_slots = n_blocks * MOE_BLOCK
    pad_lo = jnp.concatenate([pstart + counts, pend[-1:]]).astype(jnp.int32)
    pad_hi = jnp.concatenate([pend, jnp.full((1,), n_slots, pend.dtype)]).astype(jnp.int32)
    eblk = (padded // MOE_BLOCK).astype(jnp.int32)
    gcount = (eblk + GROUP_BLOCKS - 1) // GROUP_BLOCKS
    gend = jnp.cumsum(gcount)
    gstart = gend - gcount
    n_groups = (n_blocks + N_EXPERTS * (GROUP_BLOCKS - 1)) // GROUP_BLOCKS
    gid = jnp.arange(n_groups, dtype=jnp.int32)
    valid = gid < gend[-1]
    last_gid = jnp.maximum(gend[-1] - 1, 0)
    gsel = jnp.where(valid, gid, last_gid)
    group_e = jnp.minimum(jnp.sum((gsel[:, None] >= gend[None, :]).astype(jnp.int32), axis=1), N_EXPERTS - 1)
    of_group = group_e[:, None] == jnp.arange(N_EXPERTS, dtype=jnp.int32)[None, :]

    def pick(per_expert):
        return jnp.sum(jnp.where(of_group, per_expert[None, :], 0), axis=1)
    local = gsel - pick(gstart)
    group_blk = (pick(pstart) // MOE_BLOCK + local * GROUP_BLOCKS).astype(jnp.int32)
    group_n = jnp.where(valid, jnp.clip(pick(eblk) - local * GROUP_BLOCKS, 0, GROUP_BLOCKS), 0).astype(jnp.int32)
    meta = jnp.stack([gend[-1], pend[-1] // MOE_BLOCK]).astype(jnp.int32)
    return dest, n_slots, pad_lo, pad_hi, group_e.astype(jnp.int32), group_blk, group_n, meta


def _pad_lanes(v, lane0=0):
    out = jnp.zeros((1, V7X_LANES), F32)
    return out.at[0, lane0:lane0 + v.shape[0]].set(v.astype(F32))


def kernel(x, attn_norm_g, w_in, forget_bias, q_norm_g, k_norm_g, conv_w, conv_b, dt_bias, a_log, d_skip,
           ssd_norm_g, w_out, ffn_norm_g, w_router, b_router, w_gate, b_gate, w_up, b_up, w_down, b_down):
    b, s, d = x.shape
    depth = attn_norm_g.shape[0]
    att_w = ATT_HEADS * ATT_HEAD_DIM
    ssd_w = w_out.shape[1] - att_w
    ssd_heads = ssd_w // SSD_HEAD_DIM
    conv_ch = ssd_w + 2 * SSD_GROUPS * SSD_STATE
    n_tok = b * s
    dt_lane0 = ATT_HEADS
    o_f = 3 * att_w
    o_z = o_f + ATT_HEADS
    o_xbc = o_z + ssd_w
    o_dt = o_xbc + conv_ch

    x2d = x.reshape(n_tok, d)
    for layer in range(depth):
        w_all = _w_pack(w_in, layer, o_f=o_f, o_z=o_z, o_dt=o_dt, n_dt=ssd_heads)

        q, k, v, z, xbc, small = _in_proj(
            x2d, attn_norm_g[layer][None, :], w_all, q_norm_g[layer][None, :], k_norm_g[layer][None, :],
            att_w=att_w, ssd_w=ssd_w, conv_ch=conv_ch)

        small3 = small.reshape(b, s, SMALL_COLS)
        cum_s, cum_t = _forget_cum(small3, _pad_lanes(forget_bias[layer]))
        att = _attention(q.reshape(b, s, att_w), k.reshape(b, s, att_w), v.reshape(b, s, att_w), cum_s, cum_t)

        ssd = _ssd(xbc.reshape(b, s, conv_ch), z.reshape(b, s, ssd_w), small3,
                   conv_w[layer], conv_b[layer][None, :],
                   _pad_lanes(dt_bias[layer], dt_lane0), _pad_lanes(a_log[layer], dt_lane0),
                   jnp.repeat(d_skip[layer], SSD_HEAD_DIM)[None, :], ssd_norm_g[layer][None, :],
                   dt_lane0=dt_lane0)

        wr = jnp.zeros((d, V7X_LANES), F32).at[:, :N_EXPERTS].set(w_router[layer])
        wr_hi = wr.astype(BF16)
        wr_lo = (wr - wr_hi.astype(F32)).astype(BF16)
        x1, idx_pad, gate_pad = _out_proj(
            att.reshape(n_tok, att_w), ssd.reshape(n_tok, ssd_w), x2d, w_out[layer].astype(BF16),
            ffn_norm_g[layer][None, :], wr_hi, wr_lo, _pad_lanes(b_router[layer]))

        dest, n_slots, pad_lo, pad_hi, group_e, group_blk, group_n, meta = _slot_layout(idx_pad[:, :TOP_K], n_tok)
        dest = dest.reshape(-1)
        xs = _dispatch(pad_lo, pad_hi, dest, x1, ffn_norm_g[layer][None, :], n_slots)
        yb = _experts(group_e, group_blk, group_n, meta, xs,
                      w_gate[layer], b_gate[layer], w_up[layer], b_up[layer], w_down[layer], b_down[layer])
        x2d = _combine(dest, x1, gate_pad, yb)
    return x2d.reshape(b, s, d)
```

```python
import functools
import math

import jax
import jax.numpy as jnp
from jax import lax
from jax.experimental import pallas as pl
from jax.experimental.pallas import tpu as pltpu

F32 = jnp.float32
BF16 = jnp.bfloat16

ATT_HEADS = 8
ATT_HEAD_DIM = 128
SSD_HEAD_DIM = 64
SSD_GROUPS = 2
SSD_STATE = 128
CONV_WIDTH = 4
N_EXPERTS = 32
TOP_K = 4
SWIGLU_ALPHA = 1.702
SWIGLU_LIMIT = 7.0
EPS = 1e-6

V7X_LANES = 128
V7X_SUBLANES = 8
V7X_VMEM_BYTES = 64 * 1024 * 1024
VMEM_LIMIT_BYTES = 58 * 1024 * 1024

PROJ_ROWS = 256
OUT_ROWS = 512
PACK_ROWS = 256
PROJ_COLS = 512
CUM_ROWS = 256
ATT_BLOCK = 512
SSD_ROWS = 256
SSD_BATCH = 1
DISP_ROWS = 256
MOE_BLOCK = 256
GROUP_BLOCKS = 5
ZERO_ROWS = 32
FF_CHUNK = 512
COMB_ROWS = 128
SMALL_COLS = V7X_LANES


def _params(dims):
    return pltpu.CompilerParams(dimension_semantics=dims, vmem_limit_bytes=VMEM_LIMIT_BYTES)


def _rms(x, g, eps):
    return x * lax.rsqrt(jnp.mean(x * x, axis=-1, keepdims=True) + eps) * g


def _dot(a, b):
    return jnp.dot(a, b, preferred_element_type=F32)


def _cumsum_rows(x):
    n = x.shape[0]
    row = lax.broadcasted_iota(jnp.int32, x.shape, 0)
    step = 1
    while step < n:
        x = x + jnp.where(row >= step, pltpu.roll(x, step, axis=0), 0.0)
        step *= 2
    return x


def _softplus(x):
    return jnp.maximum(x, 0.0) + jnp.log(1.0 + jnp.exp(-jnp.abs(x)))


def _silu(x):
    return x * (1.0 / (1.0 + jnp.exp(-x)))


def _w_pack_body(w_ref, o_ref, *, o_f, o_z, o_dt, n_dt):
    w = w_ref[0]
    n_f = o_z - o_f
    main = o_f + (o_dt - o_z)
    o_ref[:, 0:o_f] = w[:, 0:o_f].astype(o_ref.dtype)
    o_ref[:, o_f:main] = w[:, o_z:o_dt].astype(o_ref.dtype)
    small = jnp.concatenate(
        [w[:, o_f:o_z], w[:, o_dt:o_dt + n_dt], jnp.zeros((w.shape[0], SMALL_COLS - n_f - n_dt), w.dtype)], axis=1)
    o_ref[:, main:main + SMALL_COLS] = small.astype(o_ref.dtype)


def _w_pack(w_in, layer, *, o_f, o_z, o_dt, n_dt):
    _, d, cols = w_in.shape
    out_cols = o_f + (o_dt - o_z) + SMALL_COLS
    return pl.pallas_call(
        functools.partial(_w_pack_body, o_f=o_f, o_z=o_z, o_dt=o_dt, n_dt=n_dt),
        grid=(d // PACK_ROWS,),
        in_specs=[pl.BlockSpec((1, PACK_ROWS, cols), lambda i: (layer, i, 0))],
        out_specs=pl.BlockSpec((PACK_ROWS, out_cols), lambda i: (i, 0)),
        out_shape=jax.ShapeDtypeStruct((d, out_cols), BF16),
        compiler_params=_params(("arbitrary",)),
        name="w_pack",
    )(w_in)


def _in_proj_body(x_ref, g_ref, w_ref, qg_ref, kg_ref,
                  q_ref, k_ref, v_ref, z_ref, xbc_ref, sm_ref, *, head_dim, q_scale):
    att_w = q_ref.shape[-1]
    ssd_w = z_ref.shape[-1]
    conv_ch = xbc_ref.shape[-1]
    h = _rms(x_ref[...], g_ref[...], EPS).astype(BF16)

    def proj(c0, width):
        return _dot(h, w_ref[:, c0:c0 + width])

    def chunks(width):
        return [(c, min(PROJ_COLS, width - c)) for c in range(0, width, PROJ_COLS)]

    col = 0
    for out_ref, gain_ref, scale in ((q_ref, qg_ref, q_scale), (k_ref, kg_ref, None)):
        for c0, cw in chunks(att_w):
            y = proj(col + c0, cw)
            for j in range(cw // head_dim):
                yn = _rms(y[:, j * head_dim:(j + 1) * head_dim], gain_ref[...], EPS)
                if scale is not None:
                    yn = yn * scale
                out_ref[:, c0 + j * head_dim:c0 + (j + 1) * head_dim] = yn.astype(out_ref.dtype)
        col += att_w
    for out_ref, width in ((v_ref, att_w), (z_ref, ssd_w), (xbc_ref, conv_ch), (sm_ref, SMALL_COLS)):
        for c0, cw in chunks(width):
            out_ref[:, c0:c0 + cw] = proj(col + c0, cw).astype(out_ref.dtype)
        col += width


def _in_proj(x2d, g, w_all, qg, kg, *, att_w, ssd_w, conv_ch):
    n, d = x2d.shape
    tm = PROJ_ROWS
    row = lambda i: (i, 0)
    fixed = lambda i: (0, 0)
    outs = [(att_w, BF16), (att_w, BF16), (att_w, BF16), (ssd_w, F32), (conv_ch, F32), (SMALL_COLS, F32)]
    return pl.pallas_call(
        functools.partial(_in_proj_body, head_dim=ATT_HEAD_DIM, q_scale=1.0 / math.sqrt(ATT_HEAD_DIM)),
        grid=(n // tm,),
        in_specs=[
            pl.BlockSpec((tm, d), row),
            pl.BlockSpec((1, d), fixed),
            pl.BlockSpec(w_all.shape, fixed, pipeline_mode=pl.Buffered(1)),
            pl.BlockSpec((1, ATT_HEAD_DIM), fixed),
            pl.BlockSpec((1, ATT_HEAD_DIM), fixed),
        ],
        out_specs=[pl.BlockSpec((tm, w), row) for w, _ in outs],
        out_shape=[jax.ShapeDtypeStruct((n, w), dt) for w, dt in outs],
        compiler_params=_params(("arbitrary",)),
        name="in_proj",
    )(x2d, g, w_all, qg, kg)


def _forget_cum_body(sm_ref, fb_ref, cs_ref, ct_ref, carry_ref):
    @pl.when(pl.program_id(1) == 0)
    def _():
        carry_ref[...] = jnp.zeros_like(carry_ref)

    n = sm_ref.shape[1]
    log_f = -_softplus(-(sm_ref[0] + fb_ref[...]))
    c = _cumsum_rows(log_f) + carry_ref[...]
    cs_ref[0] = c
    ct_ref[0] = c.T[0:ct_ref.shape[1], :]
    carry_ref[...] = c[n - 1:n, :]


def _forget_cum(small, fb):
    b, s, w = small.shape
    tc = CUM_ROWS
    return pl.pallas_call(
        _forget_cum_body,
        grid=(b, s // tc),
        in_specs=[pl.BlockSpec((1, tc, w), lambda i, j: (i, j, 0)),
                  pl.BlockSpec((1, w), lambda i, j: (0, 0))],
        out_specs=[pl.BlockSpec((1, tc, w), lambda i, j: (i, j, 0)),
                   pl.BlockSpec((1, V7X_SUBLANES, tc), lambda i, j: (i, 0, j))],
        out_shape=[jax.ShapeDtypeStruct((b, s, w), F32),
                   jax.ShapeDtypeStruct((b, V7X_SUBLANES, s), F32)],
        scratch_shapes=[pltpu.VMEM((1, w), F32)],
        compiler_params=_params(("arbitrary", "arbitrary")),
        name="forget_cum",
    )(small, fb)


def _attn_body(q_ref, k_ref, v_ref, cs_ref, ct_ref, o_ref):
    head = pl.program_id(1)
    s_len = q_ref.shape[1]
    t = ATT_BLOCK
    lane = lax.broadcasted_iota(jnp.int32, (t, cs_ref.shape[-1]), 1)
    row = lax.broadcasted_iota(jnp.int32, (t, t), 0)
    col = lax.broadcasted_iota(jnp.int32, (t, t), 1)
    causal = col <= row
    for qi in range(s_len // t):
        rows = slice(qi * t, (qi + 1) * t)
        q = q_ref[0, rows, :]
        cq = jnp.sum(jnp.where(lane == head, cs_ref[0, rows, :], 0.0), axis=-1, keepdims=True)
        m = l = acc = None
        for ki in range(qi + 1):
            cols = slice(ki * t, (ki + 1) * t)
            s = lax.dot_general(q, k_ref[0, cols, :], (((1,), (1,)), ((), ())), preferred_element_type=F32)
            s = (s + cq) - ct_ref[0, pl.ds(head, 1), cols]
            if ki == qi:
                s = jnp.where(causal, s, -jnp.inf)
            m_blk = jnp.max(s, axis=-1, keepdims=True)
            if ki == 0:
                m = m_blk
                pmat = jnp.exp(s - m)
                l = jnp.sum(pmat, axis=-1, keepdims=True)
                acc = _dot(pmat.astype(BF16), v_ref[0, cols, :])
            else:
                m_new = jnp.maximum(m, m_blk)
                alpha = jnp.exp(m - m_new)
                pmat = jnp.exp(s - m_new)
                l = alpha * l + jnp.sum(pmat, axis=-1, keepdims=True)
                acc = alpha * acc + _dot(pmat.astype(BF16), v_ref[0, cols, :])
                m = m_new
        o_ref[0, rows, :] = (acc / l).astype(o_ref.dtype)


def _attention(q, k, v, cum_s, cum_t):
    b, s, w = q.shape
    hd = ATT_HEAD_DIM
    heads = w // hd
    head_blk = pl.BlockSpec((1, s, hd), lambda bi, h: (bi, 0, h))
    return pl.pallas_call(
        _attn_body,
        grid=(b, heads),
        in_specs=[head_blk, head_blk, head_blk,
                  pl.BlockSpec((1, s, cum_s.shape[-1]), lambda bi, h: (bi, 0, 0)),
                  pl.BlockSpec((1, cum_t.shape[1], s), lambda bi, h: (bi, 0, 0))],
        out_specs=head_blk,
        out_shape=jax.ShapeDtypeStruct((b, s, w), BF16),
        compiler_params=_params(("arbitrary", "arbitrary")),
        name="attention",
    )(q, k, v, cum_s, cum_t)


def _ssd_body(xbc_ref, z_ref, sm_ref, cw_ref, cb_ref, dtb_ref, alog_ref, dskip_ref, ng_ref,
              o_ref, ext_ref, state_ref, y_ref, *, dt_lane0):
    @pl.when(pl.program_id(1) == 0)
    def _():
        ext_ref[...] = jnp.zeros_like(ext_ref)
        state_ref[...] = jnp.zeros_like(state_ref)

    for bi in range(xbc_ref.shape[0]):
        _ssd_one(bi, xbc_ref, z_ref, sm_ref, cw_ref, cb_ref, dtb_ref, alog_ref, dskip_ref, ng_ref,
                 o_ref, ext_ref, state_ref, y_ref, dt_lane0=dt_lane0)


def _ssd_one(bi, xbc_ref, z_ref, sm_ref, cw_ref, cb_ref, dtb_ref, alog_ref, dskip_ref, ng_ref,
             o_ref, ext_ref, state_ref, y_ref, *, dt_lane0):
    t = xbc_ref.shape[1]
    width = z_ref.shape[-1]
    lanes = V7X_LANES
    n_pairs = width // lanes
    pairs_per_group = n_pairs // SSD_GROUPS
    tail = V7X_SUBLANES

    xb = xbc_ref[bi]
    prev = ext_ref[bi]
    row8 = lax.broadcasted_iota(jnp.int32, prev.shape, 0)
    conv = cb_ref[...] + xb * cw_ref[CONV_WIDTH - 1:CONV_WIDTH, :]
    for k in range(1, CONV_WIDTH):
        rolled = pltpu.roll(xb, k, axis=0)
        head = jnp.where(row8 < k, pltpu.roll(prev, k, axis=0), rolled[0:tail, :])
        shifted = jnp.concatenate([head, rolled[tail:, :]], axis=0)
        conv = conv + shifted * cw_ref[CONV_WIDTH - 1 - k:CONV_WIDTH - k, :]
    ext_ref[bi] = xb[t - tail:t, :]
    conv = _silu(conv)

    sm = sm_ref[bi]
    dt = _softplus(sm + dtb_ref[...])
    d_a = dt * (-jnp.exp(alog_ref[...]))
    a_cs = _cumsum_rows(d_a)
    a_cs_t = a_cs.T
    a_last = a_cs[t - 1:t, :]

    row = lax.broadcasted_iota(jnp.int32, (t, t), 0)
    col = lax.broadcasted_iota(jnp.int32, (t, t), 1)
    causal = row >= col
    lane = lax.broadcasted_iota(jnp.int32, (t, lanes), 1)
    first_half = lane < SSD_HEAD_DIM
    lane1 = lax.broadcasted_iota(jnp.int32, (1, lanes), 1)

    b_off = width
    c_off = width + SSD_GROUPS * SSD_STATE
    sumsq = jnp.zeros((t, 1), F32)
    for g in range(SSD_GROUPS):
        b_g = conv[:, b_off + g * SSD_STATE:b_off + (g + 1) * SSD_STATE].astype(BF16)
        c_g = conv[:, c_off + g * SSD_STATE:c_off + (g + 1) * SSD_STATE].astype(BF16)
        cb = lax.dot_general(c_g, b_g, (((1,), (1,)), ((), ())), preferred_element_type=F32)
        for jp in range(pairs_per_group):
            pr = g * pairs_per_group + jp
            h0 = dt_lane0 + 2 * pr
            xs = conv[:, pr * lanes:(pr + 1) * lanes]
            a_pair = jnp.where(first_half, a_cs[:, h0:h0 + 1], a_cs[:, h0 + 1:h0 + 2])
            dt_pair = jnp.where(first_half, dt[:, h0:h0 + 1], dt[:, h0 + 1:h0 + 2])
            a_last_pair = jnp.where(lane1 < SSD_HEAD_DIM, a_last[:, h0:h0 + 1], a_last[:, h0 + 1:h0 + 2])
            x_dt = xs * dt_pair
            y = jnp.zeros((t, lanes), F32)
            for half in range(2):
                hh = h0 + half
                diff = a_cs[:, hh:hh + 1] - a_cs_t[hh:hh + 1, :]
                wmat = cb * jnp.exp(jnp.where(causal, diff, -jnp.inf))
                keep = first_half if half == 0 else jnp.logical_not(first_half)
                y = y + _dot(wmat.astype(BF16), jnp.where(keep, x_dt, 0.0).astype(BF16))
            st = state_ref[bi, pr]
            y = y + _dot(c_g, st.astype(BF16)) * jnp.exp(a_pair)
            x_dec = (x_dt * jnp.exp(a_last_pair - a_pair)).astype(BF16)
            upd = lax.dot_general(b_g, x_dec, (((0,), (0,)), ((), ())), preferred_element_type=F32)
            state_ref[bi, pr] = st * jnp.exp(a_last_pair) + upd
            y = y + xs * dskip_ref[:, pr * lanes:(pr + 1) * lanes]
            zz = z_ref[bi, :, pr * lanes:(pr + 1) * lanes]
            y = y * _silu(zz)
            sumsq = sumsq + jnp.sum(y * y, axis=-1, keepdims=True)
            y_ref[bi, :, pr * lanes:(pr + 1) * lanes] = y
    scale = lax.rsqrt(sumsq / width + EPS)
    o_ref[bi] = (y_ref[bi] * scale * ng_ref[...]).astype(o_ref.dtype)


def _ssd(xbc, z, small, conv_w, conv_b, dtb, alog, dskip, norm_g, *, dt_lane0):
    b, s, cc = xbc.shape
    width = z.shape[-1]
    t = SSD_ROWS
    nb = SSD_BATCH
    n_pairs = width // V7X_LANES
    blk = lambda i, j: (i, j, 0)
    fixed = lambda i, j: (0, 0)
    return pl.pallas_call(
        functools.partial(_ssd_body, dt_lane0=dt_lane0),
        grid=(b // nb, s // t),
        in_specs=[
            pl.BlockSpec((nb, t, cc), blk),
            pl.BlockSpec((nb, t, width), blk),
            pl.BlockSpec((nb, t, small.shape[-1]), blk),
            pl.BlockSpec(conv_w.shape, fixed),
            pl.BlockSpec(conv_b.shape, fixed),
            pl.BlockSpec(dtb.shape, fixed),
            pl.BlockSpec(alog.shape, fixed),
            pl.BlockSpec(dskip.shape, fixed),
            pl.BlockSpec(norm_g.shape, fixed),
        ],
        out_specs=pl.BlockSpec((nb, t, width), blk),
        out_shape=jax.ShapeDtypeStruct((b, s, width), BF16),
        scratch_shapes=[
            pltpu.VMEM((nb, V7X_SUBLANES, cc), F32),
            pltpu.VMEM((nb, n_pairs, SSD_STATE, V7X_LANES), F32),
            pltpu.VMEM((nb, t, width), F32),
        ],
        compiler_params=_params(("arbitrary", "arbitrary")),
        name="ssd",
    )(xbc, z, small, conv_w, conv_b, dtb, alog, dskip, norm_g)


def _out_proj_body(att_ref, ssd_ref, x_ref, wo_ref, g_ref, wrh_ref, wrl_ref, br_ref,
                   x1_ref, idx_ref, gate_ref):
    aw = att_ref.shape[-1]
    d = x_ref.shape[-1]
    tm = x_ref.shape[0]
    for c0 in range(0, d, PROJ_COLS):
        y = _dot(att_ref[...], wo_ref[0:aw, c0:c0 + PROJ_COLS])
        y = y + _dot(ssd_ref[...], wo_ref[aw:, c0:c0 + PROJ_COLS])
        x1_ref[:, c0:c0 + PROJ_COLS] = x_ref[:, c0:c0 + PROJ_COLS] + y
    h2 = _rms(x1_ref[...], g_ref[...], EPS)
    h_hi = h2.astype(BF16)
    h_lo = (h2 - h_hi.astype(F32)).astype(BF16)
    logits = _dot(h_hi, wrh_ref[...]) + _dot(h_lo, wrh_ref[...]) + _dot(h_hi, wrl_ref[...]) + br_ref[...]
    lane = lax.broadcasted_iota(jnp.int32, (tm, V7X_LANES), 1)
    logits = jnp.where(lane < N_EXPERTS, logits, -jnp.inf)
    lane_f = lane.astype(F32)
    idx_out = jnp.zeros((tm, V7X_LANES), F32)
    vals = []
    for kk in range(TOP_K):
        m = jnp.max(logits, axis=-1, keepdims=True)
        sel = jnp.min(jnp.where(logits == m, lane_f, float(V7X_LANES)), axis=-1, keepdims=True)
        idx_out = jnp.where(lane == kk, sel, idx_out)
        logits = jnp.where(lane_f == sel, -jnp.inf, logits)
        vals.append(m)
    exps = [jnp.exp(v - vals[0]) for v in vals]
    denom = exps[0]
    for e in exps[1:]:
        denom = denom + e
    gate_out = jnp.zeros((tm, V7X_LANES), F32)
    for kk in range(TOP_K):
        gate_out = jnp.where(lane == kk, exps[kk] / denom, gate_out)
    idx_ref[...] = idx_out.astype(jnp.int32)
    gate_ref[...] = gate_out


def _out_proj(att, ssd, x2d, wo, g, wr_hi, wr_lo, br):
    n, d = x2d.shape
    tm = OUT_ROWS
    row = lambda i: (i, 0)
    fixed = lambda i: (0, 0)
    return pl.pallas_call(
        _out_proj_body,
        grid=(n // tm,),
        in_specs=[
            pl.BlockSpec((tm, att.shape[-1]), row),
            pl.BlockSpec((tm, ssd.shape[-1]), row),
            pl.BlockSpec((tm, d), row),
            pl.BlockSpec(wo.shape, fixed, pipeline_mode=pl.Buffered(1)),
            pl.BlockSpec((1, d), fixed),
            pl.BlockSpec(wr_hi.shape, fixed),
            pl.BlockSpec(wr_lo.shape, fixed),
            pl.BlockSpec((1, V7X_LANES), fixed),
        ],
        out_specs=[pl.BlockSpec((tm, d), row),
                   pl.BlockSpec((tm, V7X_LANES), row),
                   pl.BlockSpec((tm, V7X_LANES), row)],
        out_shape=[jax.ShapeDtypeStruct((n, d), F32),
                   jax.ShapeDtypeStruct((n, V7X_LANES), jnp.int32),
                   jax.ShapeDtypeStruct((n, V7X_LANES), F32)],
        compiler_params=_params(("arbitrary",)),
        name="out_proj",
    )(att, ssd, x2d, wo, g, wr_hi, wr_lo, br)


def _dispatch_body(plo_ref, phi_ref, dest_ref, x1_ref, g_ref, xs_hbm, stage_ref, zero_ref, sems, zero_sem):
    i = pl.program_id(0)
    n = pl.num_programs(0)
    slot = i % 2
    tm = x1_ref.shape[0]
    sub = V7X_SUBLANES

    def drain(s):
        for _ in range(TOP_K):
            pltpu.make_async_copy(stage_ref.at[s], stage_ref.at[s], sems.at[s]).wait()

    @pl.when(i >= 2)
    def _():
        drain(slot)

    for r0 in range(tm // sub):
        rows = slice(r0 * sub, (r0 + 1) * sub)
        stage_ref[slot, rows, :] = _rms(x1_ref[rows, :], g_ref[...], EPS)
        for u in range(sub):
            for kk in range(TOP_K):
                dst = dest_ref[0, 0, (r0 * sub + u) * TOP_K + kk]
                pltpu.make_async_copy(stage_ref.at[slot, pl.ds(r0 * sub + u, 1), :],
                                      xs_hbm.at[pl.ds(dst, 1), :], sems.at[slot]).start()

    def pad_fill(wait):
        def go(cp):
            if wait:
                cp.wait()
            else:
                cp.start()

        def per_expert(e, carry):
            lo = plo_ref[e]
            hi = phi_ref[e]
            aligned = jnp.minimum(hi, lax.shift_left(lax.shift_right_logical(lo + (sub - 1), 3), 3))

            def one(s, c2):
                go(pltpu.make_async_copy(zero_ref.at[pl.ds(0, 1), :], xs_hbm.at[pl.ds(s, 1), :], zero_sem))
                return c2
            lax.fori_loop(lo, aligned, one, 0)

            def eight(q, c2):
                s8 = pl.multiple_of(aligned + q * sub, sub)
                go(pltpu.make_async_copy(zero_ref, xs_hbm.at[pl.ds(s8, sub), :], zero_sem))
                return c2
            lax.fori_loop(0, lax.shift_right_logical(hi - aligned, 3), eight, 0)
            return carry
        lax.fori_loop(0, plo_ref.shape[0], per_expert, 0)

    @pl.when(i == n - 1)
    def _():
        zero_ref[...] = jnp.zeros_like(zero_ref)
        pad_fill(wait=False)
        drain(slot)
        drain(1 - slot)
        pad_fill(wait=True)


def _dispatch(pad_lo, pad_hi, dest, x1, g, n_slots):
    n, d = x1.shape
    tm = DISP_ROWS
    n_tiles = n // tm
    assert n_tiles >= 2 and V7X_SUBLANES == 8
    dest3 = dest.reshape(n_tiles, 1, tm * TOP_K)
    grid_spec = pltpu.PrefetchScalarGridSpec(
        num_scalar_prefetch=2,
        grid=(n_tiles,),
        in_specs=[
            pl.BlockSpec((1, 1, tm * TOP_K), lambda i, lo, hi: (i, 0, 0), memory_space=pltpu.SMEM),
            pl.BlockSpec((tm, d), lambda i, lo, hi: (i, 0)),
            pl.BlockSpec((1, d), lambda i, lo, hi: (0, 0)),
        ],
        out_specs=pl.BlockSpec(memory_space=pl.ANY),
        scratch_shapes=[pltpu.VMEM((2, tm, d), F32),
                        pltpu.VMEM((V7X_SUBLANES, d), F32),
                        pltpu.SemaphoreType.DMA((2,)),
                        pltpu.SemaphoreType.DMA(())],
    )
    return pl.pallas_call(
        _dispatch_body,
        grid_spec=grid_spec,
        out_shape=jax.ShapeDtypeStruct((n_slots, d), F32),
        compiler_params=_params(("arbitrary",)),
        name="dispatch",
    )(pad_lo, pad_hi, dest3, x1, g)


def _experts_body(ge_ref, gb_ref, gn_ref, meta_ref, xs_hbm, wg_hbm, wu_hbm, wd_hbm, bg_ref, bu_ref, bd_ref, y_hbm,
                  wfg_ref, wfu_ref, wfd_ref, wbg_ref, wbu_ref, wbd_ref, acc_ref, xsb_ref, stage_ref, zero_ref,
                  w_sems, x_sems, out_sem, zero_sem):
    rows = MOE_BLOCK
    tf = wbg_ref.shape[-1]
    n_chunks = wg_hbm.shape[-1] // tf
    assert n_chunks % 2 == 0 and GROUP_BLOCKS <= n_chunks + 1
    n_groups = meta_ref[0]
    n_used = meta_ref[1]
    n_blocks = y_hbm.shape[0] // rows
    zrows = zero_ref.shape[0]
    last_group = n_groups - 1

    def block_rows(j):
        return pl.ds(pl.multiple_of(j * rows, rows), rows)

    def zero_fill(wait):
        def body(blk, carry):
            for q in range(rows // zrows):
                dst = pl.ds(pl.multiple_of(blk * rows + q * zrows, zrows), zrows)
                cp = pltpu.make_async_copy(zero_ref, y_hbm.at[dst, :], zero_sem)
                if wait:
                    cp.wait()
                else:
                    cp.start()
            return carry
        lax.fori_loop(n_used, n_blocks, body, 0)

    def weight_copies(e, c):
        cols = slice(c * tf, (c + 1) * tf)
        return (pltpu.make_async_copy(wg_hbm.at[e, :, cols], wfg_ref, w_sems.at[0]),
                pltpu.make_async_copy(wu_hbm.at[e, :, cols], wfu_ref, w_sems.at[1]),
                pltpu.make_async_copy(wd_hbm.at[e, cols, :], wfd_ref, w_sems.at[2]))

    def start_weights(e, c):
        for cp, prio in zip(weight_copies(e, c), (0, 1, 0)):
            cp.start(priority=prio)

    def convert_weights(slot):
        wbg_ref[slot] = wfg_ref[...].astype(BF16)
        wbu_ref[slot] = wfu_ref[...].astype(BF16)
        wbd_ref[slot] = wfd_ref[...].astype(BF16)

    def rows_copy(blk, s):
        src = pl.ds(pl.multiple_of(blk * rows, rows), rows)
        return pltpu.make_async_copy(xs_hbm.at[src, :], stage_ref.at[s], x_sems.at[s])

    def out_copy(blk0, j):
        dst = pl.ds(pl.multiple_of((blk0 + j) * rows, rows), rows)
        return pltpu.make_async_copy(acc_ref.at[block_rows(j), :], y_hbm.at[dst, :], out_sem)

    zero_ref[...] = jnp.zeros_like(zero_ref)
    zero_fill(wait=False)
    start_weights(ge_ref[0], 0)

    def first_rows(j, carry):
        cp = rows_copy(gb_ref[0] + j, 0)
        cp.start()
        cp.wait()
        xsb_ref[0, block_rows(j), :] = stage_ref[0].astype(BF16)
        return carry
    lax.fori_loop(0, gn_ref[0], first_rows, 0)
    for cp in weight_copies(ge_ref[0], 0):
        cp.wait()
    convert_weights(0)
    start_weights(ge_ref[0], 1)

    def group_body(g, prev_nblk):
        e = ge_ref[g]
        nblk = gn_ref[g]
        blk0 = gb_ref[g]
        par = g % 2
        g_next = jnp.minimum(g + 1, last_group)
        e_next = ge_ref[g_next]
        blk0_next = gb_ref[g_next]
        nblk_next = jnp.where(g < last_group, gn_ref[g_next], 0)

        def wait_out(j, carry):
            out_copy(0, j).wait()
            return carry
        lax.fori_loop(0, prev_nblk, wait_out, 0)

        for c in range(n_chunks):
            wp = c % 2
            prefetch = [c] + ([n_chunks] if c == n_chunks - 1 else [])
            for s, jb in enumerate(prefetch):
                @pl.when(jb < nblk_next)
                def _():
                    rows_copy(blk0_next + jb, s).start()

            def block(j):
                xs = xsb_ref[par, block_rows(j), :]
                gg = _dot(xs, wbg_ref[wp]) + bg_ref[pl.ds(e, 1), c * tf:(c + 1) * tf]
                uu = _dot(xs, wbu_ref[wp]) + bu_ref[pl.ds(e, 1), c * tf:(c + 1) * tf]
                gg = jnp.minimum(gg, SWIGLU_LIMIT)
                uu = jnp.clip(uu, -SWIGLU_LIMIT, SWIGLU_LIMIT)
                glu = gg * (1.0 / (1.0 + jnp.exp(-SWIGLU_ALPHA * gg)))
                act = ((uu + 1.0) * glu).astype(BF16)
                part = _dot(act, wbd_ref[wp])
                if c == 0:
                    acc_ref[block_rows(j), :] = part + bd_ref[pl.ds(e, 1), :]
                else:
                    acc_ref[block_rows(j), :] += part
                if c == n_chunks - 1:
                    out_copy(blk0, j).start()

            nxt = (e, c + 1) if c + 1 < n_chunks else (e_next, 0)
            for cp in weight_copies(*nxt):
                cp.wait()
            block(0)
            convert_weights(1 - wp)
            nxt2 = (e, c + 2) if c + 2 < n_chunks else (e_next, c + 2 - n_chunks)
            start_weights(*nxt2)

            def pair(i, carry):
                block(1 + 2 * i)
                block(2 + 2 * i)
                return carry
            n_pairs = lax.shift_right_logical(nblk - 1, 1)
            lax.fori_loop(0, n_pairs, pair, 0)

            @pl.when(2 * n_pairs + 1 < nblk)
            def _():
                block(nblk - 1)

            for s, jb in enumerate(prefetch):
                @pl.when(jb < nblk_next)
                def _():
                    rows_copy(blk0_next + jb, s).wait()
                    xsb_ref[1 - par, block_rows(jb), :] = stage_ref[s].astype(BF16)
        return nblk

    last_nblk = lax.fori_loop(0, n_groups, group_body, 0)

    def wait_out(j, carry):
        out_copy(0, j).wait()
        return carry
    lax.fori_loop(0, last_nblk, wait_out, 0)
    for cp in weight_copies(ge_ref[0], 0):
        cp.wait()
    zero_fill(wait=True)


def _experts(group_e, group_blk, group_n, meta, xs, wg, bg, wu, bu, wd, bd):
    n_slots, d = xs.shape
    dff = wg.shape[-1]
    tf = FF_CHUNK
    smem = pl.BlockSpec(memory_space=pltpu.SMEM)
    hbm = pl.BlockSpec(memory_space=pl.ANY)
    vmem = pl.BlockSpec(memory_space=pltpu.VMEM)
    return pl.pallas_call(
        _experts_body,
        in_specs=[smem, smem, smem, smem, hbm, hbm, hbm, hbm, vmem, vmem, vmem],
        out_specs=hbm,
        out_shape=jax.ShapeDtypeStruct((n_slots, d), F32),
        scratch_shapes=[pltpu.VMEM((d, tf), F32), pltpu.VMEM((d, tf), F32), pltpu.VMEM((tf, d), F32),
                        pltpu.VMEM((2, d, tf), BF16), pltpu.VMEM((2, d, tf), BF16), pltpu.VMEM((2, tf, d), BF16),
                        pltpu.VMEM((GROUP_BLOCKS * MOE_BLOCK, d), F32),
                        pltpu.VMEM((2, GROUP_BLOCKS * MOE_BLOCK, d), BF16),
                        pltpu.VMEM((2, MOE_BLOCK, d), F32),
                        pltpu.VMEM((ZERO_ROWS, d), F32),
                        pltpu.SemaphoreType.DMA((3,)),
                        pltpu.SemaphoreType.DMA((2,)),
                        pltpu.SemaphoreType.DMA(()),
                        pltpu.SemaphoreType.DMA(())],
        compiler_params=pltpu.CompilerParams(vmem_limit_bytes=VMEM_LIMIT_BYTES),
        name="experts",
    )(group_e, group_blk, group_n, meta, xs, wg, wu, wd, bg, bu, bd)


def _combine_body(dest_ref, dest_next_ref, x1_ref, gate_ref, y_hbm, o_ref, buf_ref, sems):
    tm = x1_ref.shape[0]
    i = pl.program_id(0)
    slot = i % 2

    def start_gather(ids_ref, s):
        sub = V7X_SUBLANES

        def body(r0, carry):
            base = pl.multiple_of(r0 * sub, sub)
            for u in range(sub):
                for kk in range(TOP_K):
                    src = ids_ref[0, 0, (base + u) * TOP_K + kk]
                    pltpu.make_async_copy(y_hbm.at[pl.ds(src, 1), :],
                                          buf_ref.at[s, kk, pl.ds(base + u, 1), :], sems.at[s]).start()
            return carry
        lax.fori_loop(0, tm // sub, body, 0)

    def wait_gather(s):
        pltpu.make_async_copy(buf_ref.at[s], buf_ref.at[s], sems.at[s]).wait()

    @pl.when(i == 0)
    def _():
        start_gather(dest_ref, 0)

    wait_gather(slot)
    sub = V7X_SUBLANES
    for r0 in range(tm // sub):
        rows = slice(r0 * sub, (r0 + 1) * sub)
        for u in range(sub):
            for kk in range(TOP_K):
                src = dest_next_ref[0, 0, (r0 * sub + u) * TOP_K + kk]
                pltpu.make_async_copy(y_hbm.at[pl.ds(src, 1), :],
                                      buf_ref.at[1 - slot, kk, pl.ds(r0 * sub + u, 1), :],
                                      sems.at[1 - slot]).start()
        acc = x1_ref[rows, :]
        for kk in range(TOP_K):
            acc = acc + gate_ref[rows, kk:kk + 1] * buf_ref[slot, kk, rows, :]
        o_ref[rows, :] = acc

    @pl.when(i == pl.num_programs(0) - 1)
    def _():
        wait_gather(1 - slot)


def _combine(dest, x1, gates, yb):
    n, d = x1.shape
    tm = COMB_ROWS
    n_tiles = n // tm
    dest3 = dest.reshape(n_tiles, 1, tm * TOP_K)
    row = lambda i: (i, 0)
    return pl.pallas_call(
        _combine_body,
        grid=(n_tiles,),
        in_specs=[
            pl.BlockSpec((1, 1, tm * TOP_K), lambda i: (i, 0, 0), memory_space=pltpu.SMEM),
            pl.BlockSpec((1, 1, tm * TOP_K), lambda i: (jnp.minimum(i + 1, n_tiles - 1), 0, 0),
                         memory_space=pltpu.SMEM),
            pl.BlockSpec((tm, d), row),
            pl.BlockSpec((tm, V7X_LANES), row),
            pl.BlockSpec(memory_space=pl.ANY),
        ],
        out_specs=pl.BlockSpec((tm, d), row),
        out_shape=jax.ShapeDtypeStruct((n, d), F32),
        scratch_shapes=[pltpu.VMEM((2, TOP_K, tm, d), F32), pltpu.SemaphoreType.DMA((2,))],
        compiler_params=_params(("arbitrary",)),
        name="combine",
    )(dest3, dest3, x1, gates, yb)


def _slot_layout(top_idx, n_tok):
    nk = n_tok * TOP_K
    onehot = (top_idx[:, :, None] == jnp.arange(N_EXPERTS, dtype=jnp.int32)[None, None, :])
    member = jnp.sum(onehot.astype(jnp.int32), axis=1)
    csum = jnp.cumsum(member, axis=0)
    counts = csum[-1]
    padded = ((counts + MOE_BLOCK - 1) // MOE_BLOCK) * MOE_BLOCK
    pend = jnp.cumsum(padded)
    pstart = pend - padded
    slot_te = (csum - member) + pstart[None, :]
    dest = jnp.sum(jnp.where(onehot, slot_te[:, None, :], 0), axis=-1).astype(jnp.int32)
    n_blocks = -(-nk // MOE_BLOCK) + N_EXPERTS
    n_slots = n_blocks * MOE_BLOCK
    pad_lo = jnp.concatenate([pstart + counts, pend[-1:]]).astype(jnp.int32)
    pad_hi = jnp.concatenate([pend, jnp.full((1,), n_slots, pend.dtype)]).astype(jnp.int32)
    eblk = (padded // MOE_BLOCK).astype(jnp.int32)
    gcount = (eblk + GROUP_BLOCKS - 1) // GROUP_BLOCKS
    gend = jnp.cumsum(gcount)
    gstart = gend - gcount
    n_groups = (n_blocks + N_EXPERTS * (GROUP_BLOCKS - 1)) // GROUP_BLOCKS
    gid = jnp.arange(n_groups, dtype=jnp.int32)
    valid = gid < gend[-1]
    last_gid = jnp.maximum(gend[-1] - 1, 0)
    gsel = jnp.where(valid, gid, last_gid)
    group_e = jnp.minimum(jnp.sum((gsel[:, None] >= gend[None, :]).astype(jnp.int32), axis=1), N_EXPERTS - 1)
    of_group = group_e[:, None] == jnp.arange(N_EXPERTS, dtype=jnp.int32)[None, :]

    def pick(per_expert):
        return jnp.sum(jnp.where(of_group, per_expert[None, :], 0), axis=1)
    local = gsel - pick(gstart)
    group_blk = (pick(pstart) // MOE_BLOCK + local * GROUP_BLOCKS).astype(jnp.int32)
    group_n = jnp.where(valid, jnp.clip(pick(eblk) - local * GROUP_BLOCKS, 0, GROUP_BLOCKS), 0).astype(jnp.int32)
    meta = jnp.stack([gend[-1], pend[-1] // MOE_BLOCK]).astype(jnp.int32)
    return dest, n_slots, pad_lo, pad_hi, group_e.astype(jnp.int32), group_blk, group_n, meta


def _pad_lanes(v, lane0=0):
    out = jnp.zeros((1, V7X_LANES), F32)
    return out.at[0, lane0:lane0 + v.shape[0]].set(v.astype(F32))


def kernel(x, attn_norm_g, w_in, forget_bias, q_norm_g, k_norm_g, conv_w, conv_b, dt_bias, a_log, d_skip,
           ssd_norm_g, w_out, ffn_norm_g, w_router, b_router, w_gate, b_gate, w_up, b_up, w_down, b_down):
    b, s, d = x.shape
    depth = attn_norm_g.shape[0]
    att_w = ATT_HEADS * ATT_HEAD_DIM
    ssd_w = w_out.shape[1] - att_w
    ssd_heads = ssd_w // SSD_HEAD_DIM
    conv_ch = ssd_w + 2 * SSD_GROUPS * SSD_STATE
    n_tok = b * s
    dt_lane0 = ATT_HEADS
    o_f = 3 * att_w
    o_z = o_f + ATT_HEADS
    o_xbc = o_z + ssd_w
    o_dt = o_xbc + conv_ch

    x2d = x.reshape(n_tok, d)
    for layer in range(depth):
        w_all = _w_pack(w_in, layer, o_f=o_f, o_z=o_z, o_dt=o_dt, n_dt=ssd_heads)

        q, k, v, z, xbc, small = _in_proj(
            x2d, attn_norm_g[layer][None, :], w_all, q_norm_g[layer][None, :], k_norm_g[layer][None, :],
            att_w=att_w, ssd_w=ssd_w, conv_ch=conv_ch)

        small3 = small.reshape(b, s, SMALL_COLS)
        cum_s, cum_t = _forget_cum(small3, _pad_lanes(forget_bias[layer]))
        att = _attention(q.reshape(b, s, att_w), k.reshape(b, s, att_w), v.reshape(b, s, att_w), cum_s, cum_t)

        ssd = _ssd(xbc.reshape(b, s, conv_ch), z.reshape(b, s, ssd_w), small3,
                   conv_w[layer], conv_b[layer][None, :],
                   _pad_lanes(dt_bias[layer], dt_lane0), _pad_lanes(a_log[layer], dt_lane0),
                   jnp.repeat(d_skip[layer], SSD_HEAD_DIM)[None, :], ssd_norm_g[layer][None, :],
                   dt_lane0=dt_lane0)

        wr = jnp.zeros((d, V7X_LANES), F32).at[:, :N_EXPERTS].set(w_router[layer])
        wr_hi = wr.astype(BF16)
        wr_lo = (wr - wr_hi.astype(F32)).astype(BF16)
        x1, idx_pad, gate_pad = _out_proj(
            att.reshape(n_tok, att_w), ssd.reshape(n_tok, ssd_w), x2d, w_out[layer].astype(BF16),
            ffn_norm_g[layer][None, :], wr_hi, wr_lo, _pad_lanes(b_router[layer]))

        dest, n_slots, pad_lo, pad_hi, group_e, group_blk, group_n, meta = _slot_layout(idx_pad[:, :TOP_K], n_tok)
        dest = dest.reshape(-1)
        xs = _dispatch(pad_lo, pad_hi, dest, x1, ffn_norm_g[layer][None, :], n_slots)
        yb = _experts(group_e, group_blk, group_n, meta, xs,
                      w_gate[layer], b_gate[layer], w_up[layer], b_up[layer], w_down[layer], b_down[layer])
        x2d = _combine(dest, x1, gate_pad, yb)
    return x2d.reshape(b, s, d)
```

```python
import functools
import math

import jax
import jax.numpy as jnp
from jax import lax
from jax.experimental import pallas as pl
from jax.experimental.pallas import tpu as pltpu

F32 = jnp.float32
BF16 = jnp.bfloat16

ATT_HEADS = 8
ATT_HEAD_DIM = 128
SSD_HEAD_DIM = 64
SSD_GROUPS = 2
SSD_STATE = 128
CONV_WIDTH = 4
N_EXPERTS = 32
TOP_K = 4
SWIGLU_ALPHA = 1.702
SWIGLU_LIMIT = 7.0
EPS = 1e-6

V7X_LANES = 128
V7X_SUBLANES = 8
V7X_VMEM_BYTES = 64 * 1024 * 1024
VMEM_LIMIT_BYTES = 58 * 1024 * 1024

PROJ_ROWS = 256
OUT_ROWS = 512
PACK_ROWS = 256
PROJ_COLS = 512
CUM_ROWS = 256
ATT_BLOCK = 512
SSD_ROWS = 256
SSD_BATCH = 1
DISP_ROWS = 256
MOE_BLOCK = 256
GROUP_BLOCKS = 5
ZERO_ROWS = 32
FF_CHUNK = 512
COMB_ROWS = 128
SMALL_COLS = V7X_LANES


def _params(dims):
    return pltpu.CompilerParams(dimension_semantics=dims, vmem_limit_bytes=VMEM_LIMIT_BYTES)


def _rms(x, g, eps):
    return x * lax.rsqrt(jnp.mean(x * x, axis=-1, keepdims=True) + eps) * g


def _dot(a, b):
    return jnp.dot(a, b, preferred_element_type=F32)


def _cumsum_rows(x):
    n = x.shape[0]
    row = lax.broadcasted_iota(jnp.int32, x.shape, 0)
    step = 1
    while step < n:
        x = x + jnp.where(row >= step, pltpu.roll(x, step, axis=0), 0.0)
        step *= 2
    return x


def _softplus(x):
    return jnp.maximum(x, 0.0) + jnp.log(1.0 + jnp.exp(-jnp.abs(x)))


def _silu(x):
    return x * (1.0 / (1.0 + jnp.exp(-x)))


def _w_pack_body(w_ref, o_ref, *, o_f, o_z, o_dt, n_dt):
    w = w_ref[0]
    n_f = o_z - o_f
    main = o_f + (o_dt - o_z)
    o_ref[:, 0:o_f] = w[:, 0:o_f].astype(o_ref.dtype)
    o_ref[:, o_f:main] = w[:, o_z:o_dt].astype(o_ref.dtype)
    small = jnp.concatenate(
        [w[:, o_f:o_z], w[:, o_dt:o_dt + n_dt], jnp.zeros((w.shape[0], SMALL_COLS - n_f - n_dt), w.dtype)], axis=1)
    o_ref[:, main:main + SMALL_COLS] = small.astype(o_ref.dtype)


def _w_pack(w_in, layer, *, o_f, o_z, o_dt, n_dt):
    _, d, cols = w_in.shape
    out_cols = o_f + (o_dt - o_z) + SMALL_COLS
    return pl.pallas_call(
        functools.partial(_w_pack_body, o_f=o_f, o_z=o_z, o_dt=o_dt, n_dt=n_dt),
        grid=(d // PACK_ROWS,),
        in_specs=[pl.BlockSpec((1, PACK_ROWS, cols), lambda i: (layer, i, 0))],
        out_specs=pl.BlockSpec((PACK_ROWS, out_cols), lambda i: (i, 0)),
        out_shape=jax.ShapeDtypeStruct((d, out_cols), BF16),
        compiler_params=_params(("arbitrary",)),
        name="w_pack",
    )(w_in)


def _in_proj_body(x_ref, g_ref, w_ref, qg_ref, kg_ref,
                  q_ref, k_ref, v_ref, z_ref, xbc_ref, sm_ref, *, head_dim, q_scale):
    att_w = q_ref.shape[-1]
    ssd_w = z_ref.shape[-1]
    conv_ch = xbc_ref.shape[-1]
    h = _rms(x_ref[...], g_ref[...], EPS).astype(BF16)

    def proj(c0, width):
        return _dot(h, w_ref[:, c0:c0 + width])

    def chunks(width):
        return [(c, min(PROJ_COLS, width - c)) for c in range(0, width, PROJ_COLS)]

    col = 0
    for out_ref, gain_ref, scale in ((q_ref, qg_ref, q_scale), (k_ref, kg_ref, None)):
        for c0, cw in chunks(att_w):
            y = proj(col + c0, cw)
            for j in range(cw // head_dim):
                yn = _rms(y[:, j * head_dim:(j + 1) * head_dim], gain_ref[...], EPS)
                if scale is not None:
                    yn = yn * scale
                out_ref[:, c0 + j * head_dim:c0 + (j + 1) * head_dim] = yn.astype(out_ref.dtype)
        col += att_w
    for out_ref, width in ((v_ref, att_w), (z_ref, ssd_w), (xbc_ref, conv_ch), (sm_ref, SMALL_COLS)):
        for c0, cw in chunks(width):
            out_ref[:, c0:c0 + cw] = proj(col + c0, cw).astype(out_ref.dtype)
        col += width


def _in_proj(x2d, g, w_all, qg, kg, *, att_w, ssd_w, conv_ch):
    n, d = x2d.shape
    tm = PROJ_ROWS
    row = lambda i: (i, 0)
    fixed = lambda i: (0, 0)
    outs = [(att_w, BF16), (att_w, BF16), (att_w, BF16), (ssd_w, F32), (conv_ch, F32), (SMALL_COLS, F32)]
    return pl.pallas_call(
        functools.partial(_in_proj_body, head_dim=ATT_HEAD_DIM, q_scale=1.0 / math.sqrt(ATT_HEAD_DIM)),
        grid=(n // tm,),
        in_specs=[
            pl.BlockSpec((tm, d), row),
            pl.BlockSpec((1, d), fixed),
            pl.BlockSpec(w_all.shape, fixed, pipeline_mode=pl.Buffered(1)),
            pl.BlockSpec((1, ATT_HEAD_DIM), fixed),
            pl.BlockSpec((1, ATT_HEAD_DIM), fixed),
        ],
        out_specs=[pl.BlockSpec((tm, w), row) for w, _ in outs],
        out_shape=[jax.ShapeDtypeStruct((n, w), dt) for w, dt in outs],
        compiler_params=_params(("arbitrary",)),
        name="in_proj",
    )(x2d, g, w_all, qg, kg)


def _forget_cum_body(sm_ref, fb_ref, cs_ref, ct_ref, carry_ref):
    @pl.when(pl.program_id(1) == 0)
    def _():
        carry_ref[...] = jnp.zeros_like(carry_ref)

    n = sm_ref.shape[1]
    log_f = -_softplus(-(sm_ref[0] + fb_ref[...]))
    c = _cumsum_rows(log_f) + carry_ref[...]
    cs_ref[0] = c
    ct_ref[0] = c.T[0:ct_ref.shape[1], :]
    carry_ref[...] = c[n - 1:n, :]


def _forget_cum(small, fb):
    b, s, w = small.shape
    tc = CUM_ROWS
    return pl.pallas_call(
        _forget_cum_body,
        grid=(b, s // tc),
        in_specs=[pl.BlockSpec((1, tc, w), lambda i, j: (i, j, 0)),
                  pl.BlockSpec((1, w), lambda i, j: (0, 0))],
        out_specs=[pl.BlockSpec((1, tc, w), lambda i, j: (i, j, 0)),
                   pl.BlockSpec((1, V7X_SUBLANES, tc), lambda i, j: (i, 0, j))],
        out_shape=[jax.ShapeDtypeStruct((b, s, w), F32),
                   jax.ShapeDtypeStruct((b, V7X_SUBLANES, s), F32)],
        scratch_shapes=[pltpu.VMEM((1, w), F32)],
        compiler_params=_params(("arbitrary", "arbitrary")),
        name="forget_cum",
    )(small, fb)


def _attn_body(q_ref, k_ref, v_ref, cs_ref, ct_ref, o_ref):
    head = pl.program_id(1)
    s_len = q_ref.shape[1]
    t = ATT_BLOCK
    lane = lax.broadcasted_iota(jnp.int32, (t, cs_ref.shape[-1]), 1)
    row = lax.broadcasted_iota(jnp.int32, (t, t), 0)
    col = lax.broadcasted_iota(jnp.int32, (t, t), 1)
    causal = col <= row
    for qi in range(s_len // t):
        rows = slice(qi * t, (qi + 1) * t)
        q = q_ref[0, rows, :]
        cq = jnp.sum(jnp.where(lane == head, cs_ref[0, rows, :], 0.0), axis=-1, keepdims=True)
        m = l = acc = None
        for ki in range(qi + 1):
            cols = slice(ki * t, (ki + 1) * t)
            s = lax.dot_general(q, k_ref[0, cols, :], (((1,), (1,)), ((), ())), preferred_element_type=F32)
            s = (s + cq) - ct_ref[0, pl.ds(head, 1), cols]
            if ki == qi:
                s = jnp.where(causal, s, -jnp.inf)
            m_blk = jnp.max(s, axis=-1, keepdims=True)
            if ki == 0:
                m = m_blk
                pmat = jnp.exp(s - m)
                l = jnp.sum(pmat, axis=-1, keepdims=True)
                acc = _dot(pmat.astype(BF16), v_ref[0, cols, :])
            else:
                m_new = jnp.maximum(m, m_blk)
                alpha = jnp.exp(m - m_new)
                pmat = jnp.exp(s - m_new)
                l = alpha * l + jnp.sum(pmat, axis=-1, keepdims=True)
                acc = alpha * acc + _dot(pmat.astype(BF16), v_ref[0, cols, :])
                m = m_new
        o_ref[0, rows, :] = (acc / l).astype(o_ref.dtype)


def _attention(q, k, v, cum_s, cum_t):
    b, s, w = q.shape
    hd = ATT_HEAD_DIM
    heads = w // hd
    head_blk = pl.BlockSpec((1, s, hd), lambda bi, h: (bi, 0, h))
    return pl.pallas_call(
        _attn_body,
        grid=(b, heads),
        in_specs=[head_blk, head_blk, head_blk,
                  pl.BlockSpec((1, s, cum_s.shape[-1]), lambda bi, h: (bi, 0, 0)),
                  pl.BlockSpec((1, cum_t.shape[1], s), lambda bi, h: (bi, 0, 0))],
        out_specs=head_blk,
        out_shape=jax.ShapeDtypeStruct((b, s, w), BF16),
        compiler_params=_params(("arbitrary", "arbitrary")),
        name="attention",
    )(q, k, v, cum_s, cum_t)


def _ssd_body(xbc_ref, z_ref, sm_ref, cw_ref, cb_ref, dtb_ref, alog_ref, dskip_ref, ng_ref,
              o_ref, ext_ref, state_ref, y_ref, *, dt_lane0):
    @pl.when(pl.program_id(1) == 0)
    def _():
        ext_ref[...] = jnp.zeros_like(ext_ref)
        state_ref[...] = jnp.zeros_like(state_ref)

    for bi in range(xbc_ref.shape[0]):
        _ssd_one(bi, xbc_ref, z_ref, sm_ref, cw_ref, cb_ref, dtb_ref, alog_ref, dskip_ref, ng_ref,
                 o_ref, ext_ref, state_ref, y_ref, dt_lane0=dt_lane0)


def _ssd_one(bi, xbc_ref, z_ref, sm_ref, cw_ref, cb_ref, dtb_ref, alog_ref, dskip_ref, ng_ref,
             o_ref, ext_ref, state_ref, y_ref, *, dt_lane0):
    t = xbc_ref.shape[1]
    width = z_ref.shape[-1]
    lanes = V7X_LANES
    n_pairs = width // lanes
    pairs_per_group = n_pairs // SSD_GROUPS
    tail = V7X_SUBLANES

    xb = xbc_ref[bi]
    prev = ext_ref[bi]
    row8 = lax.broadcasted_iota(jnp.int32, prev.shape, 0)
    conv = cb_ref[...] + xb * cw_ref[CONV_WIDTH - 1:CONV_WIDTH, :]
    for k in range(1, CONV_WIDTH):
        rolled = pltpu.roll(xb, k, axis=0)
        head = jnp.where(row8 < k, pltpu.roll(prev, k, axis=0), rolled[0:tail, :])
        shifted = jnp.concatenate([head, rolled[tail:, :]], axis=0)
        conv = conv + shifted * cw_ref[CONV_WIDTH - 1 - k:CONV_WIDTH - k, :]
    ext_ref[bi] = xb[t - tail:t, :]
    conv = _silu(conv)

    sm = sm_ref[bi]
    dt = _softplus(sm + dtb_ref[...])
    d_a = dt * (-jnp.exp(alog_ref[...]))
    a_cs = _cumsum_rows(d_a)
    a_cs_t = a_cs.T
    a_last = a_cs[t - 1:t, :]

    row = lax.broadcasted_iota(jnp.int32, (t, t), 0)
    col = lax.broadcasted_iota(jnp.int32, (t, t), 1)
    causal = row >= col
    lane = lax.broadcasted_iota(jnp.int32, (t, lanes), 1)
    first_half = lane < SSD_HEAD_DIM
    lane1 = lax.broadcasted_iota(jnp.int32, (1, lanes), 1)

    b_off = width
    c_off = width + SSD_GROUPS * SSD_STATE
    sumsq = jnp.zeros((t, 1), F32)
    for g in range(SSD_GROUPS):
        b_g = conv[:, b_off + g * SSD_STATE:b_off + (g + 1) * SSD_STATE].astype(BF16)
        c_g = conv[:, c_off + g * SSD_STATE:c_off + (g + 1) * SSD_STATE].astype(BF16)
        cb = lax.dot_general(c_g, b_g, (((1,), (1,)), ((), ())), preferred_element_type=F32)
        for jp in range(pairs_per_group):
            pr = g * pairs_per_group + jp
            h0 = dt_lane0 + 2 * pr
            xs = conv[:, pr * lanes:(pr + 1) * lanes]
            a_pair = jnp.where(first_half, a_cs[:, h0:h0 + 1], a_cs[:, h0 + 1:h0 + 2])
            dt_pair = jnp.where(first_half, dt[:, h0:h0 + 1], dt[:, h0 + 1:h0 + 2])
            a_last_pair = jnp.where(lane1 < SSD_HEAD_DIM, a_last[:, h0:h0 + 1], a_last[:, h0 + 1:h0 + 2])
            x_dt = xs * dt_pair
            y = jnp.zeros((t, lanes), F32)
            for half in range(2):
                hh = h0 + half
                diff = a_cs[:, hh:hh + 1] - a_cs_t[hh:hh + 1, :]
                wmat = cb * jnp.exp(jnp.where(causal, diff, -jnp.inf))
                keep = first_half if half == 0 else jnp.logical_not(first_half)
                y = y + _dot(wmat.astype(BF16), jnp.where(keep, x_dt, 0.0).astype(BF16))
            st = state_ref[bi, pr]
            y = y + _dot(c_g, st.astype(BF16)) * jnp.exp(a_pair)
            x_dec = (x_dt * jnp.exp(a_last_pair - a_pair)).astype(BF16)
            upd = lax.dot_general(b_g, x_dec, (((0,), (0,)), ((), ())), preferred_element_type=F32)
            state_ref[bi, pr] = st * jnp.exp(a_last_pair) + upd
            y = y + xs * dskip_ref[:, pr * lanes:(pr + 1) * lanes]
            zz = z_ref[bi, :, pr * lanes:(pr + 1) * lanes]
            y = y * _silu(zz)
            sumsq = sumsq + jnp.sum(y * y, axis=-1, keepdims=True)
            y_ref[bi, :, pr * lanes:(pr + 1) * lanes] = y
    scale = lax.rsqrt(sumsq / width + EPS)
    o_ref[bi] = (y_ref[bi] * scale * ng_ref[...]).astype(o_ref.dtype)


def _ssd(xbc, z, small, conv_w, conv_b, dtb, alog, dskip, norm_g, *, dt_lane0):
    b, s, cc = xbc.shape
    width = z.shape[-1]
    t = SSD_ROWS
    nb = SSD_BATCH
    n_pairs = width // V7X_LANES
    blk = lambda i, j: (i, j, 0)
    fixed = lambda i, j: (0, 0)
    return pl.pallas_call(
        functools.partial(_ssd_body, dt_lane0=dt_lane0),
        grid=(b // nb, s // t),
        in_specs=[
            pl.BlockSpec((nb, t, cc), blk),
            pl.BlockSpec((nb, t, width), blk),
            pl.BlockSpec((nb, t, small.shape[-1]), blk),
            pl.BlockSpec(conv_w.shape, fixed),
            pl.BlockSpec(conv_b.shape, fixed),
            pl.BlockSpec(dtb.shape, fixed),
            pl.BlockSpec(alog.shape, fixed),
            pl.BlockSpec(dskip.shape, fixed),
            pl.BlockSpec(norm_g.shape, fixed),
        ],
        out_specs=pl.BlockSpec((nb, t, width), blk),
        out_shape=jax.ShapeDtypeStruct((b, s, width), BF16),
        scratch_shapes=[
            pltpu.VMEM((nb, V7X_SUBLANES, cc), F32),
            pltpu.VMEM((nb, n_pairs, SSD_STATE, V7X_LANES), F32),
            pltpu.VMEM((nb, t, width), F32),
        ],
        compiler_params=_params(("arbitrary", "arbitrary")),
        name="ssd",
    )(xbc, z, small, conv_w, conv_b, dtb, alog, dskip, norm_g)


def _out_proj_body(att_ref, ssd_ref, x_ref, wo_ref, g_ref, wrh_ref, wrl_ref, br_ref,
                   x1_ref, idx_ref, gate_ref):
    aw = att_ref.shape[-1]
    d = x_ref.shape[-1]
    tm = x_ref.shape[0]
    for c0 in range(0, d, PROJ_COLS):
        y = _dot(att_ref[...], wo_ref[0:aw, c0:c0 + PROJ_COLS])
        y = y + _dot(ssd_ref[...], wo_ref[aw:, c0:c0 + PROJ_COLS])
        x1_ref[:, c0:c0 + PROJ_COLS] = x_ref[:, c0:c0 + PROJ_COLS] + y
    h2 = _rms(x1_ref[...], g_ref[...], EPS)
    h_hi = h2.astype(BF16)
    h_lo = (h2 - h_hi.astype(F32)).astype(BF16)
    logits = _dot(h_hi, wrh_ref[...]) + _dot(h_lo, wrh_ref[...]) + _dot(h_hi, wrl_ref[...]) + br_ref[...]
    lane = lax.broadcasted_iota(jnp.int32, (tm, V7X_LANES), 1)
    logits = jnp.where(lane < N_EXPERTS, logits, -jnp.inf)
    lane_f = lane.astype(F32)
    idx_out = jnp.zeros((tm, V7X_LANES), F32)
    vals = []
    for kk in range(TOP_K):
        m = jnp.max(logits, axis=-1, keepdims=True)
        sel = jnp.min(jnp.where(logits == m, lane_f, float(V7X_LANES)), axis=-1, keepdims=True)
        idx_out = jnp.where(lane == kk, sel, idx_out)
        logits = jnp.where(lane_f == sel, -jnp.inf, logits)
        vals.append(m)
    exps = [jnp.exp(v - vals[0]) for v in vals]
    denom = exps[0]
    for e in exps[1:]:
        denom = denom + e
    gate_out = jnp.zeros((tm, V7X_LANES), F32)
    for kk in range(TOP_K):
        gate_out = jnp.where(lane == kk, exps[kk] / denom, gate_out)
    idx_ref[...] = idx_out.astype(jnp.int32)
    gate_ref[...] = gate_out


def _out_proj(att, ssd, x2d, wo, g, wr_hi, wr_lo, br):
    n, d = x2d.shape
    tm = OUT_ROWS
    row = lambda i: (i, 0)
    fixed = lambda i: (0, 0)
    return pl.pallas_call(
        _out_proj_body,
        grid=(n // tm,),
        in_specs=[
            pl.BlockSpec((tm, att.shape[-1]), row),
            pl.BlockSpec((tm, ssd.shape[-1]), row),
            pl.BlockSpec((tm, d), row),
            pl.BlockSpec(wo.shape, fixed, pipeline_mode=pl.Buffered(1)),
            pl.BlockSpec((1, d), fixed),
            pl.BlockSpec(wr_hi.shape, fixed),
            pl.BlockSpec(wr_lo.shape, fixed),
            pl.BlockSpec((1, V7X_LANES), fixed),
        ],
        out_specs=[pl.BlockSpec((tm, d), row),
                   pl.BlockSpec((tm, V7X_LANES), row),
                   pl.BlockSpec((tm, V7X_LANES), row)],
        out_shape=[jax.ShapeDtypeStruct((n, d), F32),
                   jax.ShapeDtypeStruct((n, V7X_LANES), jnp.int32),
                   jax.ShapeDtypeStruct((n, V7X_LANES), F32)],
        compiler_params=_params(("arbitrary",)),
        name="out_proj",
    )(att, ssd, x2d, wo, g, wr_hi, wr_lo, br)


def _dispatch_body(plo_ref, phi_ref, dest_ref, x1_ref, g_ref, xs_hbm, stage_ref, zero_ref, sems, zero_sem):
    i = pl.program_id(0)
    n = pl.num_programs(0)
    slot = i % 2
    tm = x1_ref.shape[0]
    sub = V7X_SUBLANES

    def drain(s):
        for _ in range(TOP_K):
            pltpu.make_async_copy(stage_ref.at[s], stage_ref.at[s], sems.at[s]).wait()

    @pl.when(i >= 2)
    def _():
        drain(slot)

    for r0 in range(tm // sub):
        rows = slice(r0 * sub, (r0 + 1) * sub)
        stage_ref[slot, rows, :] = _rms(x1_ref[rows, :], g_ref[...], EPS)
        for u in range(sub):
            for kk in range(TOP_K):
                dst = dest_ref[0, 0, (r0 * sub + u) * TOP_K + kk]
                pltpu.make_async_copy(stage_ref.at[slot, pl.ds(r0 * sub + u, 1), :],
                                      xs_hbm.at[pl.ds(dst, 1), :], sems.at[slot]).start(priority=kk % 2)

    def pad_fill(wait):
        def go(cp):
            if wait:
                cp.wait()
            else:
                cp.start()

        def per_expert(e, carry):
            lo = plo_ref[e]
            hi = phi_ref[e]
            aligned = jnp.minimum(hi, lax.shift_left(lax.shift_right_logical(lo + (sub - 1), 3), 3))

            def one(s, c2):
                go(pltpu.make_async_copy(zero_ref.at[pl.ds(0, 1), :], xs_hbm.at[pl.ds(s, 1), :], zero_sem))
                return c2
            lax.fori_loop(lo, aligned, one, 0)

            def eight(q, c2):
                s8 = pl.multiple_of(aligned + q * sub, sub)
                go(pltpu.make_async_copy(zero_ref, xs_hbm.at[pl.ds(s8, sub), :], zero_sem))
                return c2
            lax.fori_loop(0, lax.shift_right_logical(hi - aligned, 3), eight, 0)
            return carry
        lax.fori_loop(0, plo_ref.shape[0], per_expert, 0)

    @pl.when(i == n - 1)
    def _():
        zero_ref[...] = jnp.zeros_like(zero_ref)
        pad_fill(wait=False)
        drain(slot)
        drain(1 - slot)
        pad_fill(wait=True)


def _dispatch(pad_lo, pad_hi, dest, x1, g, n_slots):
    n, d = x1.shape
    tm = DISP_ROWS
    n_tiles = n // tm
    assert n_tiles >= 2 and V7X_SUBLANES == 8
    dest3 = dest.reshape(n_tiles, 1, tm * TOP_K)
    grid_spec = pltpu.PrefetchScalarGridSpec(
        num_scalar_prefetch=2,
        grid=(n_tiles,),
        in_specs=[
            pl.BlockSpec((1, 1, tm * TOP_K), lambda i, lo, hi: (i, 0, 0), memory_space=pltpu.SMEM),
            pl.BlockSpec((tm, d), lambda i, lo, hi: (i, 0)),
            pl.BlockSpec((1, d), lambda i, lo, hi: (0, 0)),
        ],
        out_specs=pl.BlockSpec(memory_space=pl.ANY),
        scratch_shapes=[pltpu.VMEM((2, tm, d), F32),
                        pltpu.VMEM((V7X_SUBLANES, d), F32),
                        pltpu.SemaphoreType.DMA((2,)),
                        pltpu.SemaphoreType.DMA(())],
    )
    return pl.pallas_call(
        _dispatch_body,
        grid_spec=grid_spec,
        out_shape=jax.ShapeDtypeStruct((n_slots, d), F32),
        compiler_params=_params(("arbitrary",)),
        name="dispatch",
    )(pad_lo, pad_hi, dest3, x1, g)


def _experts_body(ge_ref, gb_ref, gn_ref, meta_ref, xs_hbm, wg_hbm, wu_hbm, wd_hbm, bg_ref, bu_ref, bd_ref, y_hbm,
                  wfg_ref, wfu_ref, wfd_ref, wbg_ref, wbu_ref, wbd_ref, acc_ref, xsb_ref, stage_ref, zero_ref,
                  w_sems, x_sems, out_sem, zero_sem):
    rows = MOE_BLOCK
    tf = wbg_ref.shape[-1]
    n_chunks = wg_hbm.shape[-1] // tf
    assert n_chunks % 2 == 0 and GROUP_BLOCKS <= n_chunks + 1
    n_groups = meta_ref[0]
    n_used = meta_ref[1]
    n_blocks = y_hbm.shape[0] // rows
    zrows = zero_ref.shape[0]
    last_group = n_groups - 1

    def block_rows(j):
        return pl.ds(pl.multiple_of(j * rows, rows), rows)

    def zero_fill(wait):
        def body(blk, carry):
            for q in range(rows // zrows):
                dst = pl.ds(pl.multiple_of(blk * rows + q * zrows, zrows), zrows)
                cp = pltpu.make_async_copy(zero_ref, y_hbm.at[dst, :], zero_sem)
                if wait:
                    cp.wait()
                else:
                    cp.start()
            return carry
        lax.fori_loop(n_used, n_blocks, body, 0)

    half = tf // 2

    def weight_copies(which, e, c):
        cols = slice(c * tf, (c + 1) * tf)
        if which == 0:
            return (pltpu.make_async_copy(wg_hbm.at[e, :, cols], wfg_ref, w_sems.at[0]),
                    pltpu.make_async_copy(wd_hbm.at[e, c * tf:c * tf + half, :],
                                          wfd_ref.at[pl.ds(0, half), :], w_sems.at[1]))
        return (pltpu.make_async_copy(wu_hbm.at[e, :, cols], wfu_ref, w_sems.at[2]),
                pltpu.make_async_copy(wd_hbm.at[e, c * tf + half:(c + 1) * tf, :],
                                      wfd_ref.at[pl.ds(half, half), :], w_sems.at[3]))

    def start_weights(which, e, c):
        for cp in weight_copies(which, e, c):
            cp.start(priority=which)

    def wait_weights(which, e, c):
        for cp in weight_copies(which, e, c):
            cp.wait()

    def convert_weights(which, slot):
        if which == 0:
            wbg_ref[slot] = wfg_ref[...].astype(BF16)
            wbd_ref[slot, 0:half, :] = wfd_ref[0:half, :].astype(BF16)
        else:
            wbu_ref[slot] = wfu_ref[...].astype(BF16)
            wbd_ref[slot, half:tf, :] = wfd_ref[half:tf, :].astype(BF16)

    def rows_copy(blk, s):
        src = pl.ds(pl.multiple_of(blk * rows, rows), rows)
        return pltpu.make_async_copy(xs_hbm.at[src, :], stage_ref.at[s], x_sems.at[s])

    def out_copy(blk0, j):
        dst = pl.ds(pl.multiple_of((blk0 + j) * rows, rows), rows)
        return pltpu.make_async_copy(acc_ref.at[block_rows(j), :], y_hbm.at[dst, :], out_sem)

    zero_ref[...] = jnp.zeros_like(zero_ref)
    zero_fill(wait=False)
    for which in (0, 1):
        start_weights(which, ge_ref[0], 0)

    def first_rows(j, carry):
        cp = rows_copy(gb_ref[0] + j, 0)
        cp.start()
        cp.wait()
        xsb_ref[0, block_rows(j), :] = stage_ref[0].astype(BF16)
        return carry
    lax.fori_loop(0, gn_ref[0], first_rows, 0)
    for which in (0, 1):
        wait_weights(which, ge_ref[0], 0)
        convert_weights(which, 0)
        start_weights(which, ge_ref[0], 1)

    def group_body(g, prev_nblk):
        e = ge_ref[g]
        nblk = gn_ref[g]
        blk0 = gb_ref[g]
        par = g % 2
        g_next = jnp.minimum(g + 1, last_group)
        e_next = ge_ref[g_next]
        blk0_next = gb_ref[g_next]
        nblk_next = jnp.where(g < last_group, gn_ref[g_next], 0)

        def wait_out(j, carry):
            out_copy(0, j).wait()
            return carry
        lax.fori_loop(0, prev_nblk, wait_out, 0)

        for c in range(n_chunks):
            wp = c % 2
            prefetch = [c] + ([n_chunks] if c == n_chunks - 1 else [])
            for s, jb in enumerate(prefetch):
                @pl.when(jb < nblk_next)
                def _():
                    rows_copy(blk0_next + jb, s).start()

            def block(j):
                xs = xsb_ref[par, block_rows(j), :]
                gg = _dot(xs, wbg_ref[wp]) + bg_ref[pl.ds(e, 1), c * tf:(c + 1) * tf]
                uu = _dot(xs, wbu_ref[wp]) + bu_ref[pl.ds(e, 1), c * tf:(c + 1) * tf]
                gg = jnp.minimum(gg, SWIGLU_LIMIT)
                uu = jnp.clip(uu, -SWIGLU_LIMIT, SWIGLU_LIMIT)
                glu = gg * (1.0 / (1.0 + jnp.exp(-SWIGLU_ALPHA * gg)))
                act = ((uu + 1.0) * glu).astype(BF16)
                part = _dot(act, wbd_ref[wp])
                if c == 0:
                    acc_ref[block_rows(j), :] = part + bd_ref[pl.ds(e, 1), :]
                else:
                    acc_ref[block_rows(j), :] += part
                if c == n_chunks - 1:
                    out_copy(blk0, j).start()

            nxt = (e, c + 1) if c + 1 < n_chunks else (e_next, 0)
            nxt2 = (e, c + 2) if c + 2 < n_chunks else (e_next, c + 2 - n_chunks)

            wait_weights(0, *nxt)
            block(0)
            convert_weights(0, 1 - wp)
            start_weights(0, *nxt2)

            @pl.when(nblk >= 2)
            def _():
                wait_weights(1, *nxt)
                block(1)
                convert_weights(1, 1 - wp)
                start_weights(1, *nxt2)

            @pl.when(nblk < 2)
            def _():
                wait_weights(1, *nxt)
                convert_weights(1, 1 - wp)
                start_weights(1, *nxt2)

            def pair(i, carry):
                block(2 + 2 * i)
                block(3 + 2 * i)
                return carry
            rest = jnp.maximum(nblk - 2, 0)
            n_pairs = lax.shift_right_logical(rest, 1)
            lax.fori_loop(0, n_pairs, pair, 0)

            @pl.when(2 * n_pairs < rest)
            def _():
                block(nblk - 1)

            for s, jb in enumerate(prefetch):
                @pl.when(jb < nblk_next)
                def _():
                    rows_copy(blk0_next + jb, s).wait()
                    xsb_ref[1 - par, block_rows(jb), :] = stage_ref[s].astype(BF16)
        return nblk

    last_nblk = lax.fori_loop(0, n_groups, group_body, 0)

    def wait_out(j, carry):
        out_copy(0, j).wait()
        return carry
    lax.fori_loop(0, last_nblk, wait_out, 0)
    for which in (0, 1):
        wait_weights(which, ge_ref[0], 0)
    zero_fill(wait=True)


def _experts(group_e, group_blk, group_n, meta, xs, wg, bg, wu, bu, wd, bd):
    n_slots, d = xs.shape
    dff = wg.shape[-1]
    tf = FF_CHUNK
    smem = pl.BlockSpec(memory_space=pltpu.SMEM)
    hbm = pl.BlockSpec(memory_space=pl.ANY)
    vmem = pl.BlockSpec(memory_space=pltpu.VMEM)
    return pl.pallas_call(
        _experts_body,
        in_specs=[smem, smem, smem, smem, hbm, hbm, hbm, hbm, vmem, vmem, vmem],
        out_specs=hbm,
        out_shape=jax.ShapeDtypeStruct((n_slots, d), F32),
        scratch_shapes=[pltpu.VMEM((d, tf), F32), pltpu.VMEM((d, tf), F32), pltpu.VMEM((tf, d), F32),
                        pltpu.VMEM((2, d, tf), BF16), pltpu.VMEM((2, d, tf), BF16), pltpu.VMEM((2, tf, d), BF16),
                        pltpu.VMEM((GROUP_BLOCKS * MOE_BLOCK, d), F32),
                        pltpu.VMEM((2, GROUP_BLOCKS * MOE_BLOCK, d), BF16),
                        pltpu.VMEM((2, MOE_BLOCK, d), F32),
                        pltpu.VMEM((ZERO_ROWS, d), F32),
                        pltpu.SemaphoreType.DMA((4,)),
                        pltpu.SemaphoreType.DMA((2,)),
                        pltpu.SemaphoreType.DMA(()),
                        pltpu.SemaphoreType.DMA(())],
        compiler_params=pltpu.CompilerParams(vmem_limit_bytes=VMEM_LIMIT_BYTES),
        name="experts",
    )(group_e, group_blk, group_n, meta, xs, wg, wu, wd, bg, bu, bd)


def _combine_body(dest_ref, dest_next_ref, x1_ref, gate_ref, y_hbm, o_ref, buf_ref, sems):
    tm = x1_ref.shape[0]
    i = pl.program_id(0)
    slot = i % 2

    def start_gather(ids_ref, s):
        sub = V7X_SUBLANES

        def body(r0, carry):
            base = pl.multiple_of(r0 * sub, sub)
            for u in range(sub):
                for kk in range(TOP_K):
                    src = ids_ref[0, 0, (base + u) * TOP_K + kk]
                    pltpu.make_async_copy(y_hbm.at[pl.ds(src, 1), :],
                                          buf_ref.at[s, kk, pl.ds(base + u, 1), :], sems.at[s]).start()
            return carry
        lax.fori_loop(0, tm // sub, body, 0)

    def wait_gather(s):
        pltpu.make_async_copy(buf_ref.at[s], buf_ref.at[s], sems.at[s]).wait()

    @pl.when(i == 0)
    def _():
        start_gather(dest_ref, 0)

    wait_gather(slot)
    sub = V7X_SUBLANES
    for r0 in range(tm // sub):
        rows = slice(r0 * sub, (r0 + 1) * sub)
        for u in range(sub):
            for kk in range(TOP_K):
                src = dest_next_ref[0, 0, (r0 * sub + u) * TOP_K + kk]
                pltpu.make_async_copy(y_hbm.at[pl.ds(src, 1), :],
                                      buf_ref.at[1 - slot, kk, pl.ds(r0 * sub + u, 1), :],
                                      sems.at[1 - slot]).start(priority=kk % 2)
        acc = x1_ref[rows, :]
        for kk in range(TOP_K):
            acc = acc + gate_ref[rows, kk:kk + 1] * buf_ref[slot, kk, rows, :]
        o_ref[rows, :] = acc

    @pl.when(i == pl.num_programs(0) - 1)
    def _():
        wait_gather(1 - slot)


def _combine(dest, x1, gates, yb):
    n, d = x1.shape
    tm = COMB_ROWS
    n_tiles = n // tm
    dest3 = dest.reshape(n_tiles, 1, tm * TOP_K)
    row = lambda i: (i, 0)
    return pl.pallas_call(
        _combine_body,
        grid=(n_tiles,),
        in_specs=[
            pl.BlockSpec((1, 1, tm * TOP_K), lambda i: (i, 0, 0), memory_space=pltpu.SMEM),
            pl.BlockSpec((1, 1, tm * TOP_K), lambda i: (jnp.minimum(i + 1, n_tiles - 1), 0, 0),
                         memory_space=pltpu.SMEM),
            pl.BlockSpec((tm, d), row),
            pl.BlockSpec((tm, V7X_LANES), row),
            pl.BlockSpec(memory_space=pl.ANY),
        ],
        out_specs=pl.BlockSpec((tm, d), row),
        out_shape=jax.ShapeDtypeStruct((n, d), F32),
        scratch_shapes=[pltpu.VMEM((2, TOP_K, tm, d), F32), pltpu.SemaphoreType.DMA((2,))],
        compiler_params=_params(("arbitrary",)),
        name="combine",
    )(dest3, dest3, x1, gates, yb)


def _slot_layout(top_idx, n_tok):
    nk = n_tok * TOP_K
    onehot = (top_idx[:, :, None] == jnp.arange(N_EXPERTS, dtype=jnp.int32)[None, None, :])
    member = jnp.sum(onehot.astype(jnp.int32), axis=1)
    csum = jnp.cumsum(member, axis=0)
    counts = csum[-1]
    padded = ((counts + MOE_BLOCK - 1) // MOE_BLOCK) * MOE_BLOCK
    pend = jnp.cumsum(padded)
    pstart = pend - padded
    slot_te = (csum - member) + pstart[None, :]
    dest = jnp.sum(jnp.where(onehot, slot_te[:, None, :], 0), axis=-1).astype(jnp.int32)
    n_blocks = -(-nk // MOE_BLOCK) + N_EXPERTS
    n_slots = n_blocks * MOE_BLOCK
    pad_lo = jnp.concatenate([pstart + counts, pend[-1:]]).astype(jnp.int32)
    pad_hi = jnp.concatenate([pend, jnp.full((1,), n_slots, pend.dtype)]).astype(jnp.int32)
    eblk = (padded // MOE_BLOCK).astype(jnp.int32)
    gcount = (eblk + GROUP_BLOCKS - 1) // GROUP_BLOCKS
    gend = jnp.cumsum(gcount)
    gstart = gend - gcount
    n_groups = (n_blocks + N_EXPERTS * (GROUP_BLOCKS - 1)) // GROUP_BLOCKS
    gid = jnp.arange(n_groups, dtype=jnp.int32)
    valid = gid < gend[-1]
    last_gid = jnp.maximum(gend[-1] - 1, 0)
    gsel = jnp.where(valid, gid, last_gid)
    group_e = jnp.minimum(jnp.sum((gsel[:, None] >= gend[None, :]).astype(jnp.int32), axis=1), N_EXPERTS - 1)
    of_group = group_e[:, None] == jnp.arange(N_EXPERTS, dtype=jnp.int32)[None, :]

    def pick(per_expert):
        return jnp.sum(jnp.where(of_group, per_expert[None, :], 0), axis=1)
    local = gsel - pick(gstart)
    group_blk = (pick(pstart) // MOE_BLOCK + local * GROUP_BLOCKS).astype(jnp.int32)
    group_n = jnp.where(valid, jnp.clip(pick(eblk) - local * GROUP_BLOCKS, 0, GROUP_BLOCKS), 0).astype(jnp.int32)
    meta = jnp.stack([gend[-1], pend[-1] // MOE_BLOCK]).astype(jnp.int32)
    return dest, n_slots, pad_lo, pad_hi, group_e.astype(jnp.int32), group_blk, group_n, meta


def _pad_lanes(v, lane0=0):
    out = jnp.zeros((1, V7X_LANES), F32)
    return out.at[0, lane0:lane0 + v.shape[0]].set(v.astype(F32))


def kernel(x, attn_norm_g, w_in, forget_bias, q_norm_g, k_norm_g, conv_w, conv_b, dt_bias, a_log, d_skip,
           ssd_norm_g, w_out, ffn_norm_g, w_router, b_router, w_gate, b_gate, w_up, b_up, w_down, b_down):
    b, s, d = x.shape
    depth = attn_norm_g.shape[0]
    att_w = ATT_HEADS * ATT_HEAD_DIM
    ssd_w = w_out.shape[1] - att_w
    ssd_heads = ssd_w // SSD_HEAD_DIM
    conv_ch = ssd_w + 2 * SSD_GROUPS * SSD_STATE
    n_tok = b * s
    dt_lane0 = ATT_HEADS
    o_f = 3 * att_w
    o_z = o_f + ATT_HEADS
    o_xbc = o_z + ssd_w
    o_dt = o_xbc + conv_ch

    x2d = x.reshape(n_tok, d)
    for layer in range(depth):
        w_all = _w_pack(w_in, layer, o_f=o_f, o_z=o_z, o_dt=o_dt, n_dt=ssd_heads)

        q, k, v, z, xbc, small = _in_proj(
            x2d, attn_norm_g[layer][None, :], w_all, q_norm_g[layer][None, :], k_norm_g[layer][None, :],
            att_w=att_w, ssd_w=ssd_w, conv_ch=conv_ch)

        small3 = small.reshape(b, s, SMALL_COLS)
        cum_s, cum_t = _forget_cum(small3, _pad_lanes(forget_bias[layer]))
        att = _attention(q.reshape(b, s, att_w), k.reshape(b, s, att_w), v.reshape(b, s, att_w), cum_s, cum_t)

        ssd = _ssd(xbc.reshape(b, s, conv_ch), z.reshape(b, s, ssd_w), small3,
                   conv_w[layer], conv_b[layer][None, :],
                   _pad_lanes(dt_bias[layer], dt_lane0), _pad_lanes(a_log[layer], dt_lane0),
                   jnp.repeat(d_skip[layer], SSD_HEAD_DIM)[None, :], ssd_norm_g[layer][None, :],
                   dt_lane0=dt_lane0)

        wr = jnp.zeros((d, V7X_LANES), F32).at[:, :N_EXPERTS].set(w_router[layer])
        wr_hi = wr.astype(BF16)
        wr_lo = (wr - wr_hi.astype(F32)).astype(BF16)
        x1, idx_pad, gate_pad = _out_proj(
            att.reshape(n_tok, att_w), ssd.reshape(n_tok, ssd_w), x2d, w_out[layer].astype(BF16),
            ffn_norm_g[layer][None, :], wr_hi, wr_lo, _pad_lanes(b_router[layer]))

        dest, n_slots, pad_lo, pad_hi, group_e, group_blk, group_n, meta = _slot_layout(idx_pad[:, :TOP_K], n_tok)
        dest = dest.reshape(-1)
        xs = _dispatch(pad_lo, pad_hi, dest, x1, ffn_norm_g[layer][None, :], n_slots)
        yb = _experts(group_e, group_blk, group_n, meta, xs,
                      w_gate[layer], b_gate[layer], w_up[layer], b_up[layer], w_down[layer], b_down[layer])
        x2d = _combine(dest, x1, gate_pad, yb)
    return x2d.reshape(b, s, d)
```

```python
import functools
import math

import jax
import jax.numpy as jnp
from jax import lax
from jax.experimental import pallas as pl
from jax.experimental.pallas import tpu as pltpu

F32 = jnp.float32
BF16 = jnp.bfloat16

ATT_HEADS = 8
ATT_HEAD_DIM = 128
SSD_HEAD_DIM = 64
SSD_GROUPS = 2
SSD_STATE = 128
CONV_WIDTH = 4
N_EXPERTS = 32
TOP_K = 4
SWIGLU_ALPHA = 1.702
SWIGLU_LIMIT = 7.0
EPS = 1e-6

V7X_LANES = 128
V7X_SUBLANES = 8
V7X_VMEM_BYTES = 64 * 1024 * 1024
VMEM_LIMIT_BYTES = 58 * 1024 * 1024

PROJ_ROWS = 256
OUT_ROWS = 512
PACK_ROWS = 128
PROJ_COLS = 512
CUM_ROWS = 256
ATT_BLOCK = 512
SSD_ROWS = 256
SSD_BATCH = 1
DISP_ROWS = 256
MOE_BLOCK = 256
GROUP_BLOCKS = 5
ZERO_ROWS = 32
FF_CHUNK = 512
COMB_ROWS = 128
SMALL_COLS = V7X_LANES


def _params(dims):
    return pltpu.CompilerParams(dimension_semantics=dims, vmem_limit_bytes=VMEM_LIMIT_BYTES)


def _rms(x, g, eps):
    return x * lax.rsqrt(jnp.mean(x * x, axis=-1, keepdims=True) + eps) * g


def _dot(a, b):
    return jnp.dot(a, b, preferred_element_type=F32)


def _cumsum_rows(x):
    n = x.shape[0]
    row = lax.broadcasted_iota(jnp.int32, x.shape, 0)
    step = 1
    while step < n:
        x = x + jnp.where(row >= step, pltpu.roll(x, step, axis=0), 0.0)
        step *= 2
    return x


def _softplus(x):
    return jnp.maximum(x, 0.0) + jnp.log(1.0 + jnp.exp(-jnp.abs(x)))


def _silu(x):
    return x * (1.0 / (1.0 + jnp.exp(-x)))


def _w_pack_body(wt_hbm, o_ref, stage_ref, sems, *, layer, o_f, o_z, o_dt, n_dt):
    j = pl.program_id(0)
    rows = o_ref.shape[0]
    n_main = (o_f + (o_dt - o_z)) // rows
    n_f = o_z - o_f
    slot = j % 2

    def fetch(jj, s):
        src = jj * rows + jnp.where(jj * rows >= o_f, n_f, 0)
        src = pl.multiple_of(src, V7X_SUBLANES)
        return pltpu.make_async_copy(wt_hbm.at[layer, pl.ds(src, rows), :], stage_ref.at[s], sems.at[s])

    @pl.when(j == 0)
    def _():
        fetch(0, 0).start()

    @pl.when(j + 1 < n_main)
    def _():
        fetch(j + 1, 1 - slot).start()

    @pl.when(j < n_main)
    def _():
        fetch(j, slot).wait()
        o_ref[...] = stage_ref[slot].astype(o_ref.dtype)

    @pl.when(j == n_main)
    def _():
        parts = (pltpu.make_async_copy(wt_hbm.at[layer, pl.ds(o_f, n_f), :],
                                       stage_ref.at[slot, pl.ds(0, n_f), :], sems.at[slot]),
                 pltpu.make_async_copy(wt_hbm.at[layer, pl.ds(o_dt, n_dt), :],
                                       stage_ref.at[slot, pl.ds(n_f, n_dt), :], sems.at[slot]))
        for cp in parts:
            cp.start()
        stage_ref[slot, n_f + n_dt:rows, :] = jnp.zeros((rows - n_f - n_dt, stage_ref.shape[-1]), stage_ref.dtype)
        for cp in parts:
            cp.wait()
        o_ref[...] = stage_ref[slot].astype(o_ref.dtype)


def _w_pack(w_in, layer, *, o_f, o_z, o_dt, n_dt):
    _, d, cols = w_in.shape
    rows = PACK_ROWS
    assert SMALL_COLS == rows and o_f % rows == 0 and (o_dt - o_z) % rows == 0
    assert (o_z - o_f) % V7X_SUBLANES == 0 and n_dt % V7X_SUBLANES == 0 and o_dt % V7X_SUBLANES == 0
    out_rows = o_f + (o_dt - o_z) + SMALL_COLS
    wt = jnp.swapaxes(w_in, 1, 2)
    return pl.pallas_call(
        functools.partial(_w_pack_body, layer=layer, o_f=o_f, o_z=o_z, o_dt=o_dt, n_dt=n_dt),
        grid=(out_rows // rows,),
        in_specs=[pl.BlockSpec(memory_space=pl.ANY)],
        out_specs=pl.BlockSpec((rows, d), lambda i: (i, 0)),
        out_shape=jax.ShapeDtypeStruct((out_rows, d), BF16),
        scratch_shapes=[pltpu.VMEM((2, rows, d), F32), pltpu.SemaphoreType.DMA((2,))],
        compiler_params=_params(("arbitrary",)),
        name="w_pack",
    )(wt)


def _in_proj_body(x_ref, g_ref, w_ref, qg_ref, kg_ref,
                  q_ref, k_ref, v_ref, z_ref, xbc_ref, sm_ref, *, head_dim, q_scale):
    att_w = q_ref.shape[-1]
    ssd_w = z_ref.shape[-1]
    conv_ch = xbc_ref.shape[-1]
    h = _rms(x_ref[...], g_ref[...], EPS).astype(BF16)

    def proj(c0, width):
        return lax.dot_general(h, w_ref[c0:c0 + width, :], (((1,), (1,)), ((), ())), preferred_element_type=F32)

    def chunks(width):
        return [(c, min(PROJ_COLS, width - c)) for c in range(0, width, PROJ_COLS)]

    col = 0
    for out_ref, gain_ref, scale in ((q_ref, qg_ref, q_scale), (k_ref, kg_ref, None)):
        for c0, cw in chunks(att_w):
            y = proj(col + c0, cw)
            for j in range(cw // head_dim):
                yn = _rms(y[:, j * head_dim:(j + 1) * head_dim], gain_ref[...], EPS)
                if scale is not None:
                    yn = yn * scale
                out_ref[:, c0 + j * head_dim:c0 + (j + 1) * head_dim] = yn.astype(out_ref.dtype)
        col += att_w
    for out_ref, width in ((v_ref, att_w), (z_ref, ssd_w), (xbc_ref, conv_ch), (sm_ref, SMALL_COLS)):
        for c0, cw in chunks(width):
            out_ref[:, c0:c0 + cw] = proj(col + c0, cw).astype(out_ref.dtype)
        col += width


def _in_proj(x2d, g, w_all, qg, kg, *, att_w, ssd_w, conv_ch):
    n, d = x2d.shape
    tm = PROJ_ROWS
    row = lambda i: (i, 0)
    fixed = lambda i: (0, 0)
    outs = [(att_w, BF16), (att_w, BF16), (att_w, BF16), (ssd_w, F32), (conv_ch, F32), (SMALL_COLS, F32)]
    return pl.pallas_call(
        functools.partial(_in_proj_body, head_dim=ATT_HEAD_DIM, q_scale=1.0 / math.sqrt(ATT_HEAD_DIM)),
        grid=(n // tm,),
        in_specs=[
            pl.BlockSpec((tm, d), row),
            pl.BlockSpec((1, d), fixed),
            pl.BlockSpec(w_all.shape, fixed, pipeline_mode=pl.Buffered(1)),
            pl.BlockSpec((1, ATT_HEAD_DIM), fixed),
            pl.BlockSpec((1, ATT_HEAD_DIM), fixed),
        ],
        out_specs=[pl.BlockSpec((tm, w), row) for w, _ in outs],
        out_shape=[jax.ShapeDtypeStruct((n, w), dt) for w, dt in outs],
        compiler_params=_params(("arbitrary",)),
        name="in_proj",
    )(x2d, g, w_all, qg, kg)


def _forget_cum_body(sm_ref, fb_ref, cs_ref, ct_ref, carry_ref):
    @pl.when(pl.program_id(1) == 0)
    def _():
        carry_ref[...] = jnp.zeros_like(carry_ref)

    n = sm_ref.shape[1]
    log_f = -_softplus(-(sm_ref[0] + fb_ref[...]))
    c = _cumsum_rows(log_f) + carry_ref[...]
    cs_ref[0] = c
    ct_ref[0] = c.T[0:ct_ref.shape[1], :]
    carry_ref[...] = c[n - 1:n, :]


def _forget_cum(small, fb):
    b, s, w = small.shape
    tc = CUM_ROWS
    return pl.pallas_call(
        _forget_cum_body,
        grid=(b, s // tc),
        in_specs=[pl.BlockSpec((1, tc, w), lambda i, j: (i, j, 0)),
                  pl.BlockSpec((1, w), lambda i, j: (0, 0))],
        out_specs=[pl.BlockSpec((1, tc, w), lambda i, j: (i, j, 0)),
                   pl.BlockSpec((1, V7X_SUBLANES, tc), lambda i, j: (i, 0, j))],
        out_shape=[jax.ShapeDtypeStruct((b, s, w), F32),
                   jax.ShapeDtypeStruct((b, V7X_SUBLANES, s), F32)],
        scratch_shapes=[pltpu.VMEM((1, w), F32)],
        compiler_params=_params(("arbitrary", "arbitrary")),
        name="forget_cum",
    )(small, fb)


def _attn_body(q_ref, k_ref, v_ref, cs_ref, ct_ref, o_ref):
    head = pl.program_id(1)
    s_len = q_ref.shape[1]
    t = ATT_BLOCK
    lane = lax.broadcasted_iota(jnp.int32, (t, cs_ref.shape[-1]), 1)
    row = lax.broadcasted_iota(jnp.int32, (t, t), 0)
    col = lax.broadcasted_iota(jnp.int32, (t, t), 1)
    causal = col <= row
    for qi in range(s_len // t):
        rows = slice(qi * t, (qi + 1) * t)
        q = q_ref[0, rows, :]
        cq = jnp.sum(jnp.where(lane == head, cs_ref[0, rows, :], 0.0), axis=-1, keepdims=True)
        m = l = acc = None
        for ki in range(qi + 1):
            cols = slice(ki * t, (ki + 1) * t)
            s = lax.dot_general(q, k_ref[0, cols, :], (((1,), (1,)), ((), ())), preferred_element_type=F32)
            s = (s + cq) - ct_ref[0, pl.ds(head, 1), cols]
            if ki == qi:
                s = jnp.where(causal, s, -jnp.inf)
            m_blk = jnp.max(s, axis=-1, keepdims=True)
            if ki == 0:
                m = m_blk
                pmat = jnp.exp(s - m)
                l = jnp.sum(pmat, axis=-1, keepdims=True)
                acc = _dot(pmat.astype(BF16), v_ref[0, cols, :])
            else:
                m_new = jnp.maximum(m, m_blk)
                alpha = jnp.exp(m - m_new)
                pmat = jnp.exp(s - m_new)
                l = alpha * l + jnp.sum(pmat, axis=-1, keepdims=True)
                acc = alpha * acc + _dot(pmat.astype(BF16), v_ref[0, cols, :])
                m = m_new
        o_ref[0, rows, :] = (acc / l).astype(o_ref.dtype)


def _attention(q, k, v, cum_s, cum_t):
    b, s, w = q.shape
    hd = ATT_HEAD_DIM
    heads = w // hd
    head_blk = pl.BlockSpec((1, s, hd), lambda bi, h: (bi, 0, h))
    return pl.pallas_call(
        _attn_body,
        grid=(b, heads),
        in_specs=[head_blk, head_blk, head_blk,
                  pl.BlockSpec((1, s, cum_s.shape[-1]), lambda bi, h: (bi, 0, 0)),
                  pl.BlockSpec((1, cum_t.shape[1], s), lambda bi, h: (bi, 0, 0))],
        out_specs=head_blk,
        out_shape=jax.ShapeDtypeStruct((b, s, w), BF16),
        compiler_params=_params(("arbitrary", "arbitrary")),
        name="attention",
    )(q, k, v, cum_s, cum_t)


def _ssd_body(xbc_ref, z_ref, sm_ref, cw_ref, cb_ref, dtb_ref, alog_ref, dskip_ref, ng_ref,
              o_ref, ext_ref, state_ref, y_ref, *, dt_lane0):
    @pl.when(pl.program_id(1) == 0)
    def _():
        ext_ref[...] = jnp.zeros_like(ext_ref)
        state_ref[...] = jnp.zeros_like(state_ref)

    for bi in range(xbc_ref.shape[0]):
        _ssd_one(bi, xbc_ref, z_ref, sm_ref, cw_ref, cb_ref, dtb_ref, alog_ref, dskip_ref, ng_ref,
                 o_ref, ext_ref, state_ref, y_ref, dt_lane0=dt_lane0)


def _ssd_one(bi, xbc_ref, z_ref, sm_ref, cw_ref, cb_ref, dtb_ref, alog_ref, dskip_ref, ng_ref,
             o_ref, ext_ref, state_ref, y_ref, *, dt_lane0):
    t = xbc_ref.shape[1]
    width = z_ref.shape[-1]
    lanes = V7X_LANES
    n_pairs = width // lanes
    pairs_per_group = n_pairs // SSD_GROUPS
    tail = V7X_SUBLANES

    xb = xbc_ref[bi]
    prev = ext_ref[bi]
    row8 = lax.broadcasted_iota(jnp.int32, prev.shape, 0)
    conv = cb_ref[...] + xb * cw_ref[CONV_WIDTH - 1:CONV_WIDTH, :]
    for k in range(1, CONV_WIDTH):
        rolled = pltpu.roll(xb, k, axis=0)
        head = jnp.where(row8 < k, pltpu.roll(prev, k, axis=0), rolled[0:tail, :])
        shifted = jnp.concatenate([head, rolled[tail:, :]], axis=0)
        conv = conv + shifted * cw_ref[CONV_WIDTH - 1 - k:CONV_WIDTH - k, :]
    ext_ref[bi] = xb[t - tail:t, :]
    conv = _silu(conv)

    sm = sm_ref[bi]
    dt = _softplus(sm + dtb_ref[...])
    d_a = dt * (-jnp.exp(alog_ref[...]))
    a_cs = _cumsum_rows(d_a)
    a_cs_t = a_cs.T
    a_last = a_cs[t - 1:t, :]

    row = lax.broadcasted_iota(jnp.int32, (t, t), 0)
    col = lax.broadcasted_iota(jnp.int32, (t, t), 1)
    causal = row >= col
    lane = lax.broadcasted_iota(jnp.int32, (t, lanes), 1)
    first_half = lane < SSD_HEAD_DIM
    lane1 = lax.broadcasted_iota(jnp.int32, (1, lanes), 1)

    b_off = width
    c_off = width + SSD_GROUPS * SSD_STATE
    sumsq = jnp.zeros((t, 1), F32)
    for g in range(SSD_GROUPS):
        b_g = conv[:, b_off + g * SSD_STATE:b_off + (g + 1) * SSD_STATE].astype(BF16)
        c_g = conv[:, c_off + g * SSD_STATE:c_off + (g + 1) * SSD_STATE].astype(BF16)
        cb = lax.dot_general(c_g, b_g, (((1,), (1,)), ((), ())), preferred_element_type=F32)
        for jp in range(pairs_per_group):
            pr = g * pairs_per_group + jp
            h0 = dt_lane0 + 2 * pr
            xs = conv[:, pr * lanes:(pr + 1) * lanes]
            a_pair = jnp.where(first_half, a_cs[:, h0:h0 + 1], a_cs[:, h0 + 1:h0 + 2])
            dt_pair = jnp.where(first_half, dt[:, h0:h0 + 1], dt[:, h0 + 1:h0 + 2])
            a_last_pair = jnp.where(lane1 < SSD_HEAD_DIM, a_last[:, h0:h0 + 1], a_last[:, h0 + 1:h0 + 2])
            x_dt = xs * dt_pair
            y = jnp.zeros((t, lanes), F32)
            for half in range(2):
                hh = h0 + half
                diff = a_cs[:, hh:hh + 1] - a_cs_t[hh:hh + 1, :]
                wmat = cb * jnp.exp(jnp.where(causal, diff, -jnp.inf))
                keep = first_half if half == 0 else jnp.logical_not(first_half)
                y = y + _dot(wmat.astype(BF16), jnp.where(keep, x_dt, 0.0).astype(BF16))
            st = state_ref[bi, pr]
            y = y + _dot(c_g, st.astype(BF16)) * jnp.exp(a_pair)
            x_dec = (x_dt * jnp.exp(a_last_pair - a_pair)).astype(BF16)
            upd = lax.dot_general(b_g, x_dec, (((0,), (0,)), ((), ())), preferred_element_type=F32)
            state_ref[bi, pr] = st * jnp.exp(a_last_pair) + upd
            y = y + xs * dskip_ref[:, pr * lanes:(pr + 1) * lanes]
            zz = z_ref[bi, :, pr * lanes:(pr + 1) * lanes]
            y = y * _silu(zz)
            sumsq = sumsq + jnp.sum(y * y, axis=-1, keepdims=True)
            y_ref[bi, :, pr * lanes:(pr + 1) * lanes] = y
    scale = lax.rsqrt(sumsq / width + EPS)
    o_ref[bi] = (y_ref[bi] * scale * ng_ref[...]).astype(o_ref.dtype)


def _ssd(xbc, z, small, conv_w, conv_b, dtb, alog, dskip, norm_g, *, dt_lane0):
    b, s, cc = xbc.shape
    width = z.shape[-1]
    t = SSD_ROWS
    nb = SSD_BATCH
    n_pairs = width // V7X_LANES
    blk = lambda i, j: (i, j, 0)
    fixed = lambda i, j: (0, 0)
    return pl.pallas_call(
        functools.partial(_ssd_body, dt_lane0=dt_lane0),
        grid=(b // nb, s // t),
        in_specs=[
            pl.BlockSpec((nb, t, cc), blk),
            pl.BlockSpec((nb, t, width), blk),
            pl.BlockSpec((nb, t, small.shape[-1]), blk),
            pl.BlockSpec(conv_w.shape, fixed),
            pl.BlockSpec(conv_b.shape, fixed),
            pl.BlockSpec(dtb.shape, fixed),
            pl.BlockSpec(alog.shape, fixed),
            pl.BlockSpec(dskip.shape, fixed),
            pl.BlockSpec(norm_g.shape, fixed),
        ],
        out_specs=pl.BlockSpec((nb, t, width), blk),
        out_shape=jax.ShapeDtypeStruct((b, s, width), BF16),
        scratch_shapes=[
            pltpu.VMEM((nb, V7X_SUBLANES, cc), F32),
            pltpu.VMEM((nb, n_pairs, SSD_STATE, V7X_LANES), F32),
            pltpu.VMEM((nb, t, width), F32),
        ],
        compiler_params=_params(("arbitrary", "arbitrary")),
        name="ssd",
    )(xbc, z, small, conv_w, conv_b, dtb, alog, dskip, norm_g)


def _out_proj_body(att_ref, ssd_ref, x_ref, wo_ref, g_ref, wrh_ref, wrl_ref, br_ref,
                   x1_ref, idx_ref, gate_ref):
    aw = att_ref.shape[-1]
    d = x_ref.shape[-1]
    tm = x_ref.shape[0]
    for c0 in range(0, d, PROJ_COLS):
        y = _dot(att_ref[...], wo_ref[0:aw, c0:c0 + PROJ_COLS])
        y = y + _dot(ssd_ref[...], wo_ref[aw:, c0:c0 + PROJ_COLS])
        x1_ref[:, c0:c0 + PROJ_COLS] = x_ref[:, c0:c0 + PROJ_COLS] + y
    h2 = _rms(x1_ref[...], g_ref[...], EPS)
    h_hi = h2.astype(BF16)
    h_lo = (h2 - h_hi.astype(F32)).astype(BF16)
    logits = _dot(h_hi, wrh_ref[...]) + _dot(h_lo, wrh_ref[...]) + _dot(h_hi, wrl_ref[...]) + br_ref[...]
    lane = lax.broadcasted_iota(jnp.int32, (tm, V7X_LANES), 1)
    logits = jnp.where(lane < N_EXPERTS, logits, -jnp.inf)
    lane_f = lane.astype(F32)
    idx_out = jnp.zeros((tm, V7X_LANES), F32)
    vals = []
    for kk in range(TOP_K):
        m = jnp.max(logits, axis=-1, keepdims=True)
        sel = jnp.min(jnp.where(logits == m, lane_f, float(V7X_LANES)), axis=-1, keepdims=True)
        idx_out = jnp.where(lane == kk, sel, idx_out)
        logits = jnp.where(lane_f == sel, -jnp.inf, logits)
        vals.append(m)
    exps = [jnp.exp(v - vals[0]) for v in vals]
    denom = exps[0]
    for e in exps[1:]:
        denom = denom + e
    gate_out = jnp.zeros((tm, V7X_LANES), F32)
    for kk in range(TOP_K):
        gate_out = jnp.where(lane == kk, exps[kk] / denom, gate_out)
    idx_ref[...] = idx_out.astype(jnp.int32)
    gate_ref[...] = gate_out


def _out_proj(att, ssd, x2d, wo, g, wr_hi, wr_lo, br):
    n, d = x2d.shape
    tm = OUT_ROWS
    row = lambda i: (i, 0)
    fixed = lambda i: (0, 0)
    return pl.pallas_call(
        _out_proj_body,
        grid=(n // tm,),
        in_specs=[
            pl.BlockSpec((tm, att.shape[-1]), row),
            pl.BlockSpec((tm, ssd.shape[-1]), row),
            pl.BlockSpec((tm, d), row),
            pl.BlockSpec(wo.shape, fixed, pipeline_mode=pl.Buffered(1)),
            pl.BlockSpec((1, d), fixed),
            pl.BlockSpec(wr_hi.shape, fixed),
            pl.BlockSpec(wr_lo.shape, fixed),
            pl.BlockSpec((1, V7X_LANES), fixed),
        ],
        out_specs=[pl.BlockSpec((tm, d), row),
                   pl.BlockSpec((tm, V7X_LANES), row),
                   pl.BlockSpec((tm, V7X_LANES), row)],
        out_shape=[jax.ShapeDtypeStruct((n, d), F32),
                   jax.ShapeDtypeStruct((n, V7X_LANES), jnp.int32),
                   jax.ShapeDtypeStruct((n, V7X_LANES), F32)],
        compiler_params=_params(("arbitrary",)),
        name="out_proj",
    )(att, ssd, x2d, wo, g, wr_hi, wr_lo, br)


def _dispatch_body(plo_ref, phi_ref, dest_ref, x1_ref, g_ref, xs_hbm, stage_ref, zero_ref, sems, zero_sem):
    i = pl.program_id(0)
    n = pl.num_programs(0)
    slot = i % 2
    tm = x1_ref.shape[0]
    sub = V7X_SUBLANES

    def drain(s):
        for _ in range(TOP_K):
            pltpu.make_async_copy(stage_ref.at[s], stage_ref.at[s], sems.at[s]).wait()

    @pl.when(i >= 2)
    def _():
        drain(slot)

    for r0 in range(tm // sub):
        rows = slice(r0 * sub, (r0 + 1) * sub)
        stage_ref[slot, rows, :] = _rms(x1_ref[rows, :], g_ref[...], EPS)
        for u in range(sub):
            for kk in range(TOP_K):
                dst = dest_ref[0, 0, (r0 * sub + u) * TOP_K + kk]
                pltpu.make_async_copy(stage_ref.at[slot, pl.ds(r0 * sub + u, 1), :],
                                      xs_hbm.at[pl.ds(dst, 1), :], sems.at[slot]).start(priority=kk % 2)

    def pad_fill(wait):
        def go(cp):
            if wait:
                cp.wait()
            else:
                cp.start()

        def per_expert(e, carry):
            lo = plo_ref[e]
            hi = phi_ref[e]
            aligned = jnp.minimum(hi, lax.shift_left(lax.shift_right_logical(lo + (sub - 1), 3), 3))

            def one(s, c2):
                go(pltpu.make_async_copy(zero_ref.at[pl.ds(0, 1), :], xs_hbm.at[pl.ds(s, 1), :], zero_sem))
                return c2
            lax.fori_loop(lo, aligned, one, 0)

            def eight(q, c2):
                s8 = pl.multiple_of(aligned + q * sub, sub)
                go(pltpu.make_async_copy(zero_ref, xs_hbm.at[pl.ds(s8, sub), :], zero_sem))
                return c2
            lax.fori_loop(0, lax.shift_right_logical(hi - aligned, 3), eight, 0)
            return carry
        lax.fori_loop(0, plo_ref.shape[0], per_expert, 0)

    @pl.when(i == n - 1)
    def _():
        zero_ref[...] = jnp.zeros_like(zero_ref)
        pad_fill(wait=False)
        drain(slot)
        drain(1 - slot)
        pad_fill(wait=True)


def _dispatch(pad_lo, pad_hi, dest, x1, g, n_slots):
    n, d = x1.shape
    tm = DISP_ROWS
    n_tiles = n // tm
    assert n_tiles >= 2 and V7X_SUBLANES == 8
    dest3 = dest.reshape(n_tiles, 1, tm * TOP_K)
    grid_spec = pltpu.PrefetchScalarGridSpec(
        num_scalar_prefetch=2,
        grid=(n_tiles,),
        in_specs=[
            pl.BlockSpec((1, 1, tm * TOP_K), lambda i, lo, hi: (i, 0, 0), memory_space=pltpu.SMEM),
            pl.BlockSpec((tm, d), lambda i, lo, hi: (i, 0)),
            pl.BlockSpec((1, d), lambda i, lo, hi: (0, 0)),
        ],
        out_specs=pl.BlockSpec(memory_space=pl.ANY),
        scratch_shapes=[pltpu.VMEM((2, tm, d), F32),
                        pltpu.VMEM((V7X_SUBLANES, d), F32),
                        pltpu.SemaphoreType.DMA((2,)),
                        pltpu.SemaphoreType.DMA(())],
    )
    return pl.pallas_call(
        _dispatch_body,
        grid_spec=grid_spec,
        out_shape=jax.ShapeDtypeStruct((n_slots, d), F32),
        compiler_params=_params(("arbitrary",)),
        name="dispatch",
    )(pad_lo, pad_hi, dest3, x1, g)


def _experts_body(ge_ref, gb_ref, gn_ref, meta_ref, xs_hbm, wg_hbm, wu_hbm, wd_hbm, bg_ref, bu_ref, bd_ref, y_hbm,
                  wfg_ref, wfu_ref, wfd_ref, wbg_ref, wbu_ref, wbd_ref, acc_ref, xsb_ref, stage_ref, zero_ref,
                  w_sems, x_sems, out_sem, zero_sem):
    rows = MOE_BLOCK
    tf = wbg_ref.shape[-1]
    n_chunks = wg_hbm.shape[-1] // tf
    assert n_chunks % 2 == 0 and GROUP_BLOCKS <= n_chunks + 1
    n_groups = meta_ref[0]
    n_used = meta_ref[1]
    n_blocks = y_hbm.shape[0] // rows
    zrows = zero_ref.shape[0]
    last_group = n_groups - 1

    def block_rows(j):
        return pl.ds(pl.multiple_of(j * rows, rows), rows)

    def zero_fill(wait):
        def body(blk, carry):
            for q in range(rows // zrows):
                dst = pl.ds(pl.multiple_of(blk * rows + q * zrows, zrows), zrows)
                cp = pltpu.make_async_copy(zero_ref, y_hbm.at[dst, :], zero_sem)
                if wait:
                    cp.wait()
                else:
                    cp.start()
            return carry
        lax.fori_loop(n_used, n_blocks, body, 0)

    half = tf // 2

    def weight_copies(which, e, c):
        cols = slice(c * tf, (c + 1) * tf)
        if which == 0:
            return (pltpu.make_async_copy(wg_hbm.at[e, :, cols], wfg_ref, w_sems.at[0]),
                    pltpu.make_async_copy(wd_hbm.at[e, c * tf:c * tf + half, :],
                                          wfd_ref.at[pl.ds(0, half), :], w_sems.at[1]))
        return (pltpu.make_async_copy(wu_hbm.at[e, :, cols], wfu_ref, w_sems.at[2]),
                pltpu.make_async_copy(wd_hbm.at[e, c * tf + half:(c + 1) * tf, :],
                                      wfd_ref.at[pl.ds(half, half), :], w_sems.at[3]))

    def start_weights(which, e, c):
        for cp in weight_copies(which, e, c):
            cp.start(priority=which)

    def wait_weights(which, e, c):
        for cp in weight_copies(which, e, c):
            cp.wait()

    def convert_weights(which, slot):
        if which == 0:
            wbg_ref[slot] = wfg_ref[...].astype(BF16)
            wbd_ref[slot, 0:half, :] = wfd_ref[0:half, :].astype(BF16)
        else:
            wbu_ref[slot] = wfu_ref[...].astype(BF16)
            wbd_ref[slot, half:tf, :] = wfd_ref[half:tf, :].astype(BF16)

    def rows_copy(blk, s):
        src = pl.ds(pl.multiple_of(blk * rows, rows), rows)
        return pltpu.make_async_copy(xs_hbm.at[src, :], stage_ref.at[s], x_sems.at[s])

    def out_copy(blk0, j):
        dst = pl.ds(pl.multiple_of((blk0 + j) * rows, rows), rows)
        return pltpu.make_async_copy(acc_ref.at[block_rows(j), :], y_hbm.at[dst, :], out_sem)

    zero_ref[...] = jnp.zeros_like(zero_ref)
    zero_fill(wait=False)
    for which in (0, 1):
        start_weights(which, ge_ref[0], 0)

    def first_rows(j, carry):
        cp = rows_copy(gb_ref[0] + j, 0)
        cp.start()
        cp.wait()
        xsb_ref[0, block_rows(j), :] = stage_ref[0].astype(BF16)
        return carry
    lax.fori_loop(0, gn_ref[0], first_rows, 0)
    for which in (0, 1):
        wait_weights(which, ge_ref[0], 0)
        convert_weights(which, 0)
        start_weights(which, ge_ref[0], 1)

    def group_body(g, prev_nblk):
        e = ge_ref[g]
        nblk = gn_ref[g]
        blk0 = gb_ref[g]
        par = g % 2
        g_next = jnp.minimum(g + 1, last_group)
        e_next = ge_ref[g_next]
        blk0_next = gb_ref[g_next]
        nblk_next = jnp.where(g < last_group, gn_ref[g_next], 0)

        def wait_out(j, carry):
            out_copy(0, j).wait()
            return carry
        lax.fori_loop(0, prev_nblk, wait_out, 0)

        for c in range(n_chunks):
            wp = c % 2
            prefetch = [c] + ([n_chunks] if c == n_chunks - 1 else [])
            for s, jb in enumerate(prefetch):
                @pl.when(jb < nblk_next)
                def _():
                    rows_copy(blk0_next + jb, s).start()

            def block(j):
                xs = xsb_ref[par, block_rows(j), :]
                gg = _dot(xs, wbg_ref[wp]) + bg_ref[pl.ds(e, 1), c * tf:(c + 1) * tf]
                uu = _dot(xs, wbu_ref[wp]) + bu_ref[pl.ds(e, 1), c * tf:(c + 1) * tf]
                gg = jnp.minimum(gg, SWIGLU_LIMIT)
                uu = jnp.clip(uu, -SWIGLU_LIMIT, SWIGLU_LIMIT)
                glu = gg * (1.0 / (1.0 + jnp.exp(-SWIGLU_ALPHA * gg)))
                act = ((uu + 1.0) * glu).astype(BF16)
                part = _dot(act, wbd_ref[wp])
                if c == 0:
                    acc_ref[block_rows(j), :] = part + bd_ref[pl.ds(e, 1), :]
                else:
                    acc_ref[block_rows(j), :] += part
                if c == n_chunks - 1:
                    out_copy(blk0, j).start()

            nxt = (e, c + 1) if c + 1 < n_chunks else (e_next, 0)
            nxt2 = (e, c + 2) if c + 2 < n_chunks else (e_next, c + 2 - n_chunks)

            wait_weights(0, *nxt)
            block(0)
            convert_weights(0, 1 - wp)
            start_weights(0, *nxt2)

            @pl.when(nblk >= 2)
            def _():
                wait_weights(1, *nxt)
                block(1)
                convert_weights(1, 1 - wp)
                start_weights(1, *nxt2)

            @pl.when(nblk < 2)
            def _():
                wait_weights(1, *nxt)
                convert_weights(1, 1 - wp)
                start_weights(1, *nxt2)

            def pair(i, carry):
                block(2 + 2 * i)
                block(3 + 2 * i)
                return carry
            rest = jnp.maximum(nblk - 2, 0)
            n_pairs = lax.shift_right_logical(rest, 1)
            lax.fori_loop(0, n_pairs, pair, 0)

            @pl.when(2 * n_pairs < rest)
            def _():
                block(nblk - 1)

            for s, jb in enumerate(prefetch):
                @pl.when(jb < nblk_next)
                def _():
                    rows_copy(blk0_next + jb, s).wait()
                    xsb_ref[1 - par, block_rows(jb), :] = stage_ref[s].astype(BF16)
        return nblk

    last_nblk = lax.fori_loop(0, n_groups, group_body, 0)

    def wait_out(j, carry):
        out_copy(0, j).wait()
        return carry
    lax.fori_loop(0, last_nblk, wait_out, 0)
    for which in (0, 1):
        wait_weights(which, ge_ref[0], 0)
    zero_fill(wait=True)


def _experts(group_e, group_blk, group_n, meta, xs, wg, bg, wu, bu, wd, bd):
    n_slots, d = xs.shape
    dff = wg.shape[-1]
    tf = FF_CHUNK
    smem = pl.BlockSpec(memory_space=pltpu.SMEM)
    hbm = pl.BlockSpec(memory_space=pl.ANY)
    vmem = pl.BlockSpec(memory_space=pltpu.VMEM)
    return pl.pallas_call(
        _experts_body,
        in_specs=[smem, smem, smem, smem, hbm, hbm, hbm, hbm, vmem, vmem, vmem],
        out_specs=hbm,
        out_shape=jax.ShapeDtypeStruct((n_slots, d), F32),
        scratch_shapes=[pltpu.VMEM((d, tf), F32), pltpu.VMEM((d, tf), F32), pltpu.VMEM((tf, d), F32),
                        pltpu.VMEM((2, d, tf), BF16), pltpu.VMEM((2, d, tf), BF16), pltpu.VMEM((2, tf, d), BF16),
                        pltpu.VMEM((GROUP_BLOCKS * MOE_BLOCK, d), F32),
                        pltpu.VMEM((2, GROUP_BLOCKS * MOE_BLOCK, d), BF16),
                        pltpu.VMEM((2, MOE_BLOCK, d), F32),
                        pltpu.VMEM((ZERO_ROWS, d), F32),
                        pltpu.SemaphoreType.DMA((4,)),
                        pltpu.SemaphoreType.DMA((2,)),
                        pltpu.SemaphoreType.DMA(()),
                        pltpu.SemaphoreType.DMA(())],
        compiler_params=pltpu.CompilerParams(vmem_limit_bytes=VMEM_LIMIT_BYTES),
        name="experts",
    )(group_e, group_blk, group_n, meta, xs, wg, wu, wd, bg, bu, bd)


def _combine_body(dest_ref, dest_next_ref, x1_ref, gate_ref, y_hbm, o_ref, buf_ref, sems):
    tm = x1_ref.shape[0]
    i = pl.program_id(0)
    slot = i % 2

    def start_gather(ids_ref, s):
        sub = V7X_SUBLANES

        def body(r0, carry):
            base = pl.multiple_of(r0 * sub, sub)
            for u in range(sub):
                for kk in range(TOP_K):
                    src = ids_ref[0, 0, (base + u) * TOP_K + kk]
                    pltpu.make_async_copy(y_hbm.at[pl.ds(src, 1), :],
                                          buf_ref.at[s, kk, pl.ds(base + u, 1), :], sems.at[s]).start()
            return carry
        lax.fori_loop(0, tm // sub, body, 0)

    def wait_gather(s):
        pltpu.make_async_copy(buf_ref.at[s], buf_ref.at[s], sems.at[s]).wait()

    @pl.when(i == 0)
    def _():
        start_gather(dest_ref, 0)

    wait_gather(slot)
    sub = V7X_SUBLANES
    for r0 in range(tm // sub):
        rows = slice(r0 * sub, (r0 + 1) * sub)
        for u in range(sub):
            for kk in range(TOP_K):
                src = dest_next_ref[0, 0, (r0 * sub + u) * TOP_K + kk]
                pltpu.make_async_copy(y_hbm.at[pl.ds(src, 1), :],
                                      buf_ref.at[1 - slot, kk, pl.ds(r0 * sub + u, 1), :],
                                      sems.at[1 - slot]).start(priority=kk % 2)
        acc = x1_ref[rows, :]
        for kk in range(TOP_K):
            acc = acc + gate_ref[rows, kk:kk + 1] * buf_ref[slot, kk, rows, :]
        o_ref[rows, :] = acc

    @pl.when(i == pl.num_programs(0) - 1)
    def _():
        wait_gather(1 - slot)


def _combine(dest, x1, gates, yb):
    n, d = x1.shape
    tm = COMB_ROWS
    n_tiles = n // tm
    dest3 = dest.reshape(n_tiles, 1, tm * TOP_K)
    row = lambda i: (i, 0)
    return pl.pallas_call(
        _combine_body,
        grid=(n_tiles,),
        in_specs=[
            pl.BlockSpec((1, 1, tm * TOP_K), lambda i: (i, 0, 0), memory_space=pltpu.SMEM),
            pl.BlockSpec((1, 1, tm * TOP_K), lambda i: (jnp.minimum(i + 1, n_tiles - 1), 0, 0),
                         memory_space=pltpu.SMEM),
            pl.BlockSpec((tm, d), row),
            pl.BlockSpec((tm, V7X_LANES), row),
            pl.BlockSpec(memory_space=pl.ANY),
        ],
        out_specs=pl.BlockSpec((tm, d), row),
        out_shape=jax.ShapeDtypeStruct((n, d), F32),
        scratch_shapes=[pltpu.VMEM((2, TOP_K, tm, d), F32), pltpu.SemaphoreType.DMA((2,))],
        compiler_params=_params(("arbitrary",)),
        name="combine",
    )(dest3, dest3, x1, gates, yb)


def _slot_layout(top_idx, n_tok):
    nk = n_tok * TOP_K
    onehot = (top_idx[:, :, None] == jnp.arange(N_EXPERTS, dtype=jnp.int32)[None, None, :])
    member = jnp.sum(onehot.astype(jnp.int32), axis=1)
    csum = jnp.cumsum(member, axis=0)
    counts = csum[-1]
    padded = ((counts + MOE_BLOCK - 1) // MOE_BLOCK) * MOE_BLOCK
    pend = jnp.cumsum(padded)
    pstart = pend - padded
    slot_te = (csum - member) + pstart[None, :]
    dest = jnp.sum(jnp.where(onehot, slot_te[:, None, :], 0), axis=-1).astype(jnp.int32)
    n_blocks = -(-nk // MOE_BLOCK) + N_EXPERTS
    n_slots = n_blocks * MOE_BLOCK
    pad_lo = jnp.concatenate([pstart + counts, pend[-1:]]).astype(jnp.int32)
    pad_hi = jnp.concatenate([pend, jnp.full((1,), n_slots, pend.dtype)]).astype(jnp.int32)
    eblk = (padded // MOE_BLOCK).astype(jnp.int32)
    gcount = (eblk + GROUP_BLOCKS - 1) // GROUP_BLOCKS
    gend = jnp.cumsum(gcount)
    gstart = gend - gcount
    n_groups = (n_blocks + N_EXPERTS * (GROUP_BLOCKS - 1)) // GROUP_BLOCKS
    gid = jnp.arange(n_groups, dtype=jnp.int32)
    valid = gid < gend[-1]
    last_gid = jnp.maximum(gend[-1] - 1, 0)
    gsel = jnp.where(valid, gid, last_gid)
    group_e = jnp.minimum(jnp.sum((gsel[:, None] >= gend[None, :]).astype(jnp.int32), axis=1), N_EXPERTS - 1)
    of_group = group_e[:, None] == jnp.arange(N_EXPERTS, dtype=jnp.int32)[None, :]

    def pick(per_expert):
        return jnp.sum(jnp.where(of_group, per_expert[None, :], 0), axis=1)
    local = gsel - pick(gstart)
    group_blk = (pick(pstart) // MOE_BLOCK + local * GROUP_BLOCKS).astype(jnp.int32)
    group_n = jnp.where(valid, jnp.clip(pick(eblk) - local * GROUP_BLOCKS, 0, GROUP_BLOCKS), 0).astype(jnp.int32)
    meta = jnp.stack([gend[-1], pend[-1] // MOE_BLOCK]).astype(jnp.int32)
    return dest, n_slots, pad_lo, pad_hi, group_e.astype(jnp.int32), group_blk, group_n, meta


def _pad_lanes(v, lane0=0):
    out = jnp.zeros((1, V7X_LANES), F32)
    return out.at[0, lane0:lane0 + v.shape[0]].set(v.astype(F32))


def kernel(x, attn_norm_g, w_in, forget_bias, q_norm_g, k_norm_g, conv_w, conv_b, dt_bias, a_log, d_skip,
           ssd_norm_g, w_out, ffn_norm_g, w_router, b_router, w_gate, b_gate, w_up, b_up, w_down, b_down):
    b, s, d = x.shape
    depth = attn_norm_g.shape[0]
    att_w = ATT_HEADS * ATT_HEAD_DIM
    ssd_w = w_out.shape[1] - att_w
    ssd_heads = ssd_w // SSD_HEAD_DIM
    conv_ch = ssd_w + 2 * SSD_GROUPS * SSD_STATE
    n_tok = b * s
    dt_lane0 = ATT_HEADS
    o_f = 3 * att_w
    o_z = o_f + ATT_HEADS
    o_xbc = o_z + ssd_w
    o_dt = o_xbc + conv_ch

    x2d = x.reshape(n_tok, d)
    for layer in range(depth):
        w_all = _w_pack(w_in, layer, o_f=o_f, o_z=o_z, o_dt=o_dt, n_dt=ssd_heads)

        q, k, v, z, xbc, small = _in_proj(
            x2d, attn_norm_g[layer][None, :], w_all, q_norm_g[layer][None, :], k_norm_g[layer][None, :],
            att_w=att_w, ssd_w=ssd_w, conv_ch=conv_ch)

        small3 = small.reshape(b, s, SMALL_COLS)
        cum_s, cum_t = _forget_cum(small3, _pad_lanes(forget_bias[layer]))
        att = _attention(q.reshape(b, s, att_w), k.reshape(b, s, att_w), v.reshape(b, s, att_w), cum_s, cum_t)

        ssd = _ssd(xbc.reshape(b, s, conv_ch), z.reshape(b, s, ssd_w), small3,
                   conv_w[layer], conv_b[layer][None, :],
                   _pad_lanes(dt_bias[layer], dt_lane0), _pad_lanes(a_log[layer], dt_lane0),
                   jnp.repeat(d_skip[layer], SSD_HEAD_DIM)[None, :], ssd_norm_g[layer][None, :],
                   dt_lane0=dt_lane0)

        wr = jnp.zeros((d, V7X_LANES), F32).at[:, :N_EXPERTS].set(w_router[layer])
        wr_hi = wr.astype(BF16)
        wr_lo = (wr - wr_hi.astype(F32)).astype(BF16)
        x1, idx_pad, gate_pad = _out_proj(
            att.reshape(n_tok, att_w), ssd.reshape(n_tok, ssd_w), x2d, w_out[layer].astype(BF16),
            ffn_norm_g[layer][None, :], wr_hi, wr_lo, _pad_lanes(b_router[layer]))

        dest, n_slots, pad_lo, pad_hi, group_e, group_blk, group_n, meta = _slot_layout(idx_pad[:, :TOP_K], n_tok)
        dest = dest.reshape(-1)
        xs = _dispatch(pad_lo, pad_hi, dest, x1, ffn_norm_g[layer][None, :], n_slots)
        yb = _experts(group_e, group_blk, group_n, meta, xs,
                      w_gate[layer], b_gate[layer], w_up[layer], b_up[layer], w_down[layer], b_down[layer])
        x2d = _combine(dest, x1, gate_pad, yb)
    return x2d.reshape(b, s, d)
```
